```python
import math
import jax, jax.numpy as jnp
from jax import lax
import numpy as np

D_MODEL = 1024
BATCH = 16
SEQ = 2048
DEPTH = 4

MIX_WIDTH = 2 * D_MODEL
SSD_WIDTH = MIX_WIDTH // 2
SSD_HEAD_DIM = 64
SSD_HEADS = SSD_WIDTH // SSD_HEAD_DIM
SSD_GROUPS = 2
SSD_HEADS_PER_GROUP = SSD_HEADS // SSD_GROUPS
SSD_STATE = 128
SSD_CONV = 4
SSD_CHUNK = 128
SSD_CONV_DIM = SSD_WIDTH + 2 * SSD_GROUPS * SSD_STATE
HGRN_WIDTH = MIX_WIDTH // 4
HGRN_HEAD_DIM = 128
HGRN_HEADS = HGRN_WIDTH // HGRN_HEAD_DIM
HGRN_CHUNK = 64
S5_WIDTH = MIX_WIDTH - SSD_WIDTH - HGRN_WIDTH
S5_GROUP_SIZE = 16
S5_GROUPS = S5_WIDTH // S5_GROUP_SIZE
S5_STATE = 64
S5_MIN_NEG = 1e-4

IN_COLS = SSD_WIDTH + SSD_CONV_DIM + SSD_HEADS + 4 * HGRN_WIDTH + 2 * S5_WIDTH
EPS = 1e-6

kernel_name = "hybrid_ssd_hgrn2_s5_parallel_heads"


def _in_proj_splits():
    widths = [SSD_WIDTH, SSD_CONV_DIM, SSD_HEADS, HGRN_WIDTH, HGRN_WIDTH,
              HGRN_WIDTH, HGRN_WIDTH, S5_WIDTH, S5_WIDTH]
    pts, acc = [], 0
    for w in widths[:-1]:
        acc += w
        pts.append(acc)
    return pts


def rms_norm(x, w):
    xf = x.astype(jnp.float32)
    y = xf * lax.rsqrt(jnp.mean(xf * xf, axis=-1, keepdims=True) + EPS)
    return (y * w.astype(jnp.float32)).astype(x.dtype)


def grouped_rms_norm(x, w, n_groups):
    shp = x.shape
    xf = x.astype(jnp.float32).reshape(shp[:-1] + (n_groups, shp[-1] // n_groups))
    y = xf * lax.rsqrt(jnp.mean(xf * xf, axis=-1, keepdims=True) + EPS)
    return (y.reshape(shp) * w.astype(jnp.float32)).astype(x.dtype)


def causal_depthwise_conv(x, w, b):
    out = lax.conv_general_dilated(
        x, w[:, None, :].astype(x.dtype), window_strides=(1,),
        padding=[(SSD_CONV - 1, 0)], dimension_numbers=("NWC", "WIO", "NWC"),
        feature_group_count=x.shape[-1])
    return out + b.astype(x.dtype)


def ssd_mixer(z, xbc, dt_raw, conv_w, conv_b, dt_bias, a_log, d_skip, norm_w):
    bsz, seqlen, _ = xbc.shape
    nc = seqlen // SSD_CHUNK
    G, R, P, N, T = SSD_GROUPS, SSD_HEADS_PER_GROUP, SSD_HEAD_DIM, SSD_STATE, SSD_CHUNK
    xbc = jax.nn.silu(causal_depthwise_conv(xbc, conv_w, conv_b))
    xs, b_in, c_in = jnp.split(xbc, [SSD_WIDTH, SSD_WIDTH + G * N], axis=-1)
    xs = xs.reshape(bsz, nc, T, G, R, P)
    b_in = b_in.reshape(bsz, nc, T, G, N)
    c_in = c_in.reshape(bsz, nc, T, G, N)
    dt = jax.nn.softplus(dt_raw.astype(jnp.float32) + dt_bias.astype(jnp.float32))
    dt = dt.reshape(bsz, nc, T, G, R)
    a = -jnp.exp(a_log.astype(jnp.float32)).reshape(G, R)
    a_cum = jnp.cumsum(dt * a, axis=2)
    mask = jnp.tril(jnp.ones((T, T), dtype=bool))[:, :, None, None]
    seg = a_cum[:, :, :, None] - a_cum[:, :, None, :]
    decay = jnp.exp(jnp.where(mask, seg, -jnp.inf))
    x_dt = xs * dt[..., None]
    scores = jnp.einsum("bclgn,bcsgn->bclsg", c_in, b_in)
    y_diag = jnp.einsum("bclsg,bclsgr,bcsgrp->bclgrp", scores, decay, x_dt)
    decay_to_end = jnp.exp(a_cum[:, :, -1:] - a_cum)
    chunk_states = jnp.einsum("bclgn,bclgr,bclgrp->bcgrpn", b_in, decay_to_end, x_dt)
    chunk_decay = jnp.exp(a_cum[:, :, -1])

    def step(h, inp):
        dec, st = inp
        return dec[..., None, None] * h + st, h

    h0 = jnp.zeros((bsz, G, R, P, N), chunk_states.dtype)
    _, prev = lax.scan(step, h0, (jnp.moveaxis(chunk_decay, 1, 0),
                                  jnp.moveaxis(chunk_states, 1, 0)))
    prev = jnp.moveaxis(prev, 0, 1)
    y_off = jnp.einsum("bclgn,bcgrpn,bclgr->bclgrp", c_in, prev, jnp.exp(a_cum))
    y = y_diag + y_off + d_skip.reshape(G, R)[:, :, None] * xs
    y = y.reshape(bsz, seqlen, SSD_WIDTH)
    return grouped_rms_norm(y * jax.nn.silu(z), norm_w, SSD_GROUPS)


def hgrn2_mixer(q, f_raw, i_in, g, lb, norm_w):
    bsz, seqlen, _ = q.shape
    nc = seqlen // HGRN_CHUNK
    H, Dh, T = HGRN_HEADS, HGRN_HEAD_DIM, HGRN_CHUNK
    q = jax.nn.silu(q)
    lbf = lb.astype(jnp.float32)
    log_f = jnp.logaddexp(jnp.log(lbf), jnp.log1p(-lbf) + jax.nn.log_sigmoid(f_raw.astype(jnp.float32)))
    k = -jnp.expm1(log_f)

    def to_chunks(t):
        return t.reshape(bsz, nc, T, H, Dh).transpose(1, 0, 3, 2, 4)

    mask = jnp.tril(jnp.ones((T, T), dtype=bool))[:, :, None]

    def step(state, inp):
        qc, kc, vc, lfc = inp
        cum = jnp.cumsum(lfc, axis=2)
        decay = jnp.exp(jnp.where(mask, cum[:, :, :, None, :] - cum[:, :, None, :, :], -jnp.inf))
        attn = jnp.einsum("bhlk,bhsk,bhlsk->bhls", qc, kc, decay)
        o = (jnp.einsum("bhls,bhsv->bhlv", attn, vc)
             + jnp.einsum("bhlk,bhkv->bhlv", qc * jnp.exp(cum), state))
        last = cum[:, :, -1:, :]
        state = (jnp.exp(last[:, :, 0, :])[..., None] * state
                 + jnp.einsum("bhsk,bhsv->bhkv", kc * jnp.exp(last - cum), vc))
        return state, o

    state0 = jnp.zeros((bsz, H, Dh, Dh), jnp.float32)
    _, o = lax.scan(step, state0, (to_chunks(q), to_chunks(k), to_chunks(i_in), to_chunks(log_f)))
    o = o.transpose(1, 0, 3, 2, 4).reshape(bsz, seqlen, HGRN_WIDTH)
    return grouped_rms_norm(o, norm_w, HGRN_HEADS) * jax.nn.silu(g)


def s5_mixer(u, gate, lam_re, lam_im, log_step, b_re, b_im, c_re, c_im, d_skip, glu_w, glu_b, norm_w):
    bsz, seqlen, _ = u.shape
    G, P, Hg = S5_GROUPS, S5_STATE, S5_GROUP_SIZE
    uf = u.astype(jnp.float32)
    lr = jnp.minimum(lam_re.astype(jnp.float32), -S5_MIN_NEG)
    li = lam_im.astype(jnp.float32)
    step = jnp.exp(log_step.astype(jnp.float32))[:, None]
    mag = jnp.exp(lr * step)
    ang = li * step
    ab_re, ab_im = mag * jnp.cos(ang), mag * jnp.sin(ang)
    den = lr * lr + li * li
    num_re, num_im = ab_re - 1.0, ab_im
    co_re = (num_re * lr + num_im * li) / den
    co_im = (num_im * lr - num_re * li) / den
    bb_re = co_re[..., None] * b_re - co_im[..., None] * b_im
    bb_im = co_re[..., None] * b_im + co_im[..., None] * b_re
    ug = uf.reshape(bsz, seqlen, G, Hg)
    bu_re = jnp.einsum("blgh,gph->lbgp", ug, bb_re)
    bu_im = jnp.einsum("blgh,gph->lbgp", ug, bb_im)
    a_re = jnp.broadcast_to(ab_re, (seqlen, 1, G, P))
    a_im = jnp.broadcast_to(ab_im, (seqlen, 1, G, P))

    def combine(e_i, e_j):
        ar_i, ai_i, br_i, bi_i = e_i
        ar_j, ai_j, br_j, bi_j = e_j
        return (ar_j * ar_i - ai_j * ai_i,
                ar_j * ai_i + ai_j * ar_i,
                ar_j * br_i - ai_j * bi_i + br_j,
                ar_j * bi_i + ai_j * br_i + bi_j)

    _, _, s_re, s_im = lax.associative_scan(combine, (a_re, a_im, bu_re, bu_im), axis=0)
    y = jnp.einsum("lbgp,ghp->blgh", s_re, c_re) - jnp.einsum("lbgp,ghp->blgh", s_im, c_im)
    y = y.reshape(bsz, seqlen, S5_WIDTH) + d_skip * uf
    y = jax.nn.gelu(y)
    hg = y @ glu_w + glu_b
    val, gt = jnp.split(hg, 2, axis=-1)
    y = val * jax.nn.sigmoid(gt) * jax.nn.silu(gate)
    return rms_norm(y, norm_w)


def setup_inputs(seed: int = 0) -> dict:
    key = jax.random.key(seed)
    ks = jax.random.split(key, 24)
    f32 = jnp.float32
    nrm = lambda k, shp, s: s * jax.random.normal(k, shp, f32)
    dt0 = jnp.exp(jax.random.uniform(ks[6], (DEPTH, SSD_HEADS), f32, math.log(1e-3), math.log(1e-1)))
    lam_im0 = math.pi * jnp.arange(S5_STATE, dtype=f32)
    return {
        "x": jax.random.normal(ks[0], (BATCH, SEQ, D_MODEL), f32),
        "pre_norm_w": 1.0 + nrm(ks[1], (DEPTH, D_MODEL), 0.02),
        "post_norm_w": 1.0 + nrm(ks[2], (DEPTH, D_MODEL), 0.02),
        "w_in": nrm(ks[3], (DEPTH, D_MODEL, IN_COLS), D_MODEL ** -0.5),
        "w_out": nrm(ks[4], (DEPTH, MIX_WIDTH, D_MODEL), MIX_WIDTH ** -0.5),
        "ssd_conv_w": nrm(ks[5], (DEPTH, SSD_CONV, SSD_CONV_DIM), SSD_CONV ** -0.5),
        "ssd_conv_b": nrm(ks[7], (DEPTH, SSD_CONV_DIM), 0.02),
        "ssd_dt_bias": dt0 + jnp.log(-jnp.expm1(-dt0)),
        "ssd_a_log": jnp.log(jax.random.uniform(ks[8], (DEPTH, SSD_HEADS), f32, 1.0, 16.0)),
        "ssd_d": 1.0 + nrm(ks[9], (DEPTH, SSD_HEADS), 0.02),
        "ssd_norm_w": 1.0 + nrm(ks[10], (DEPTH, SSD_WIDTH), 0.02),
        "hgrn_lower_bounds": nrm(ks[11], (DEPTH, HGRN_WIDTH), 0.1),
        "hgrn_norm_w": 1.0 + nrm(ks[12], (DEPTH, HGRN_WIDTH), 0.02),
        "s5_lambda_re": -0.5 + nrm(ks[13], (DEPTH, S5_GROUPS, S5_STATE), 0.01),
        "s5_lambda_im": lam_im0 + nrm(ks[14], (DEPTH, S5_GROUPS, S5_STATE), 0.01),
        "s5_log_step": jax.random.uniform(ks[15], (DEPTH, S5_GROUPS), f32, math.log(1e-3), math.log(1e-1)),
        "s5_b_re": nrm(ks[16], (DEPTH, S5_GROUPS, S5_STATE, S5_GROUP_SIZE), (2 * S5_GROUP_SIZE) ** -0.5),
        "s5_b_im": nrm(ks[17], (DEPTH, S5_GROUPS, S5_STATE, S5_GROUP_SIZE), (2 * S5_GROUP_SIZE) ** -0.5),
        "s5_c_re": nrm(ks[18], (DEPTH, S5_GROUPS, S5_GROUP_SIZE, S5_STATE), S5_STATE ** -0.5),
        "s5_c_im": nrm(ks[19], (DEPTH, S5_GROUPS, S5_GROUP_SIZE, S5_STATE), S5_STATE ** -0.5),
        "s5_d": jax.random.normal(ks[20], (DEPTH, S5_WIDTH), f32),
        "s5_glu_w": nrm(ks[21], (DEPTH, S5_WIDTH, 2 * S5_WIDTH), S5_WIDTH ** -0.5),
        "s5_glu_b": nrm(ks[22], (DEPTH, 2 * S5_WIDTH), 0.02),
        "s5_norm_w": 1.0 + nrm(ks[23], (DEPTH, S5_WIDTH), 0.02),
    }


def reference(x, pre_norm_w, post_norm_w, w_in, w_out, ssd_conv_w, ssd_conv_b, ssd_dt_bias,
              ssd_a_log, ssd_d, ssd_norm_w, hgrn_lower_bounds, hgrn_norm_w, s5_lambda_re,
              s5_lambda_im, s5_log_step, s5_b_re, s5_b_im, s5_c_re, s5_c_im, s5_d, s5_glu_w,
              s5_glu_b, s5_norm_w):
    lb_all = jnp.cumsum(jax.nn.softmax(hgrn_lower_bounds.astype(jnp.float32), axis=0), axis=0)
    lb_all = lb_all - lb_all[0:1]
    splits = _in_proj_splits()
    for l in range(DEPTH):
        h = rms_norm(x, pre_norm_w[l])
        proj = h @ w_in[l]
        z, xbc, dt_raw, q, f_raw, i_in, g, u, s5_gate = jnp.split(proj, splits, axis=-1)
        y_ssd = ssd_mixer(z, xbc, dt_raw, ssd_conv_w[l], ssd_conv_b[l], ssd_dt_bias[l],
                          ssd_a_log[l], ssd_d[l], ssd_norm_w[l])
        y_hgrn = hgrn2_mixer(q, f_raw, i_in, g, lb_all[l], hgrn_norm_w[l])
        y_s5 = s5_mixer(u, s5_gate, s5_lambda_re[l], s5_lambda_im[l], s5_log_step[l],
                        s5_b_re[l], s5_b_im[l], s5_c_re[l], s5_c_im[l], s5_d[l],
                        s5_glu_w[l], s5_glu_b[l], s5_norm_w[l])
        mix = jnp.concatenate([y_ssd, y_hgrn, y_s5], axis=-1).astype(x.dtype)
        x = x + rms_norm(mix @ w_out[l], post_norm_w[l])
    return x
```

```python
import functools

import jax
import jax.numpy as jnp
from jax import lax
from jax.experimental import pallas as pl
from jax.experimental.pallas import tpu as pltpu

F32 = jnp.float32
BF16 = jnp.bfloat16
EPS = 1e-6

D_MODEL = 1024
LANES = 128
SUBLANES = 8
MXU_DIM = 256

SSD_WIDTH = 1024
SSD_HEADS = 16
SSD_HEAD_DIM = 64
SSD_GROUPS = 2
SSD_STATE = 128
SSD_CHUNK = 128
SSD_CONV = 4
SSD_XBC = SSD_WIDTH + 2 * SSD_GROUPS * SSD_STATE
SSD_DT_PAD = LANES
SSD_COLS = SSD_WIDTH + SSD_XBC + SSD_DT_PAD
SSD_PAIRS = SSD_HEADS // 2

HGRN_WIDTH = 512
HGRN_HEADS = 4
HGRN_HEAD_DIM = 128
HGRN_CHUNK = 64
HGRN_COLS = 4 * HGRN_WIDTH

S5_WIDTH = 512
S5_GROUPS = 32
S5_GROUP_SIZE = 16
S5_STATE = 64
S5_MIN_NEG = 1e-4
S5_COLS = 2 * S5_WIDTH
S5_TILES = S5_GROUPS // 2
S5_NSTATE = S5_TILES * MXU_DIM
S5_STRIP = 4

ROW_TB = 8
ROW_TT = 64
SSD_TT = 256
HGRN_TT = 256
S5_TT = 32
VMEM_LIMIT = 56 * 1024 * 1024


def _sigmoid(x):
    return 1.0 / (1.0 + jnp.exp(-x))


def _silu(x):
    return x * _sigmoid(x)


def _dot(a, b):
    return jnp.dot(a, b, preferred_element_type=F32)


def _dot_nt(a, b):
    return lax.dot_general(a, b, (((1,), (1,)), ((), ())), preferred_element_type=F32)


def _dot_tn(a, b):
    return lax.dot_general(a, b, (((0,), (0,)), ((), ())), preferred_element_type=F32)


def _dot_exact01(m01, x):
    hi = x.astype(BF16)
    r1 = x - hi.astype(F32)
    mid = r1.astype(BF16)
    lo = (r1 - mid.astype(F32)).astype(BF16)
    return _dot(m01, hi) + _dot(m01, mid) + _dot(m01, lo)


def _iota(shape, axis):
    return lax.broadcasted_iota(jnp.int32, shape, axis)


def _cumsum8(x, rows_mod8):
    for sh in (1, 2, 4):
        x = x + jnp.where(rows_mod8 >= sh, pltpu.roll(x, sh, axis=0), 0.0)
    return x


def _inproj_kernel(x_ref, nw_ref, wssd_ref, whg_ref, ws5_ref, ossd_ref, ohg_ref, os5_ref):
    nb, tt, d = x_ref.shape
    x = x_ref[...].reshape(nb * tt, d)
    ms = jnp.mean(x * x, axis=-1, keepdims=True)
    h = (x * lax.rsqrt(ms + EPS) * nw_ref[...]).astype(BF16)
    ossd_ref[...] = _dot(h, wssd_ref[...]).reshape(nb, tt, SSD_COLS)
    ohg_ref[...] = _dot(h, whg_ref[...]).reshape(nb, tt, HGRN_COLS)
    s5 = _dot(h, ws5_ref[...])
    for b in range(nb):
        os5_ref[:, b, :] = s5[b * tt:(b + 1) * tt, :]


def _inproj(x, nw, wssd, whg, ws5):
    bsz, seqlen, d = x.shape
    const = lambda shape: pl.BlockSpec(shape, lambda i, j: (0,) * len(shape),
                                       pipeline_mode=pl.Buffered(1))
    return pl.pallas_call(
        _inproj_kernel,
        grid=(bsz // ROW_TB, seqlen // ROW_TT),
        in_specs=[
            pl.BlockSpec((ROW_TB, ROW_TT, d), lambda i, j: (i, j, 0)),
            const((1, d)),
            const((d, SSD_COLS)),
            const((d, HGRN_COLS)),
            const((d, S5_COLS)),
        ],
        out_specs=[
            pl.BlockSpec((ROW_TB, ROW_TT, SSD_COLS), lambda i, j: (i, j, 0)),
            pl.BlockSpec((ROW_TB, ROW_TT, HGRN_COLS), lambda i, j: (i, j, 0)),
            pl.BlockSpec((ROW_TT, ROW_TB, S5_COLS), lambda i, j: (j, i, 0)),
        ],
        out_shape=[
            jax.ShapeDtypeStruct((bsz, seqlen, SSD_COLS), F32),
            jax.ShapeDtypeStruct((bsz, seqlen, HGRN_COLS), F32),
            jax.ShapeDtypeStruct((seqlen, bsz, S5_COLS), F32),
        ],
        compiler_params=pltpu.CompilerParams(
            dimension_semantics=("parallel", "parallel"), vmem_limit_bytes=VMEM_LIMIT),
        name="inproj",
    )(x, nw, wssd, whg, ws5)


def _outproj_kernel(x_ref, yssd_ref, yhg_ref, ys5_ref, wo_ref, pw_ref, o_ref):
    nb, tt, d = x_ref.shape
    rows = nb * tt
    x = x_ref[...].reshape(rows, d)
    ys = yssd_ref[...].reshape(rows, SSD_WIDTH)
    yh = yhg_ref[...].reshape(rows, HGRN_WIDTH)
    y5 = jnp.concatenate([ys5_ref[:, b, :] for b in range(nb)], axis=0).astype(BF16)
    m = (_dot(ys, wo_ref[0:SSD_WIDTH, :])
         + _dot(yh, wo_ref[SSD_WIDTH:SSD_WIDTH + HGRN_WIDTH, :])
         + _dot(y5, wo_ref[SSD_WIDTH + HGRN_WIDTH:, :]))
    ms = jnp.mean(m * m, axis=-1, keepdims=True)
    o_ref[...] = (x + m * lax.rsqrt(ms + EPS) * pw_ref[...]).reshape(nb, tt, d)


def _outproj(x, yssd, yhg, ys5, wo, pw):
    bsz, seqlen, d = x.shape
    const = lambda shape: pl.BlockSpec(shape, lambda i, j: (0,) * len(shape),
                                       pipeline_mode=pl.Buffered(1))
    return pl.pallas_call(
        _outproj_kernel,
        grid=(bsz // ROW_TB, seqlen // ROW_TT),
        in_specs=[
            pl.BlockSpec((ROW_TB, ROW_TT, d), lambda i, j: (i, j, 0)),
            pl.BlockSpec((ROW_TB, ROW_TT, SSD_WIDTH), lambda i, j: (i, j, 0)),
            pl.BlockSpec((ROW_TB, ROW_TT, HGRN_WIDTH), lambda i, j: (i, j, 0)),
            pl.BlockSpec((ROW_TT, ROW_TB, S5_WIDTH), lambda i, j: (j, i, 0)),
            const((2 * d, d)),
            const((1, d)),
        ],
        out_specs=pl.BlockSpec((ROW_TB, ROW_TT, d), lambda i, j: (i, j, 0)),
        out_shape=jax.ShapeDtypeStruct((bsz, seqlen, d), F32),
        compiler_params=pltpu.CompilerParams(
            dimension_semantics=("parallel", "parallel"), vmem_limit_bytes=VMEM_LIMIT),
        name="outproj",
    )(x, yssd, yhg, ys5, wo, pw)


def _ssd_kernel(in_ref, cw_ref, cb_ref, dtb_ref, a_ref, dsk_ref, nw_ref, o_ref,
                xbc_buf, act_scr, dt_scr, y_buf, hst):
    tt = in_ref.shape[1]
    T = SSD_CHUNK

    @pl.when(pl.program_id(1) == 0)
    def _():
        xbc_buf[0:SUBLANES, :] = jnp.zeros((SUBLANES, SSD_XBC), F32)
        hst[...] = jnp.zeros(hst.shape, F32)

    xbc_buf[SUBLANES:SUBLANES + tt, :] = in_ref[0, :, SSD_WIDTH:SSD_WIDTH + SSD_XBC]
    conv = cb_ref[...]
    for k in range(SSD_CONV):
        off = SUBLANES - (SSD_CONV - 1) + k
        conv = conv + cw_ref[k:k + 1, :] * xbc_buf[off:off + tt, :]
    xbc_buf[0:SUBLANES, :] = xbc_buf[tt:tt + SUBLANES, :]
    act_scr[...] = _silu(conv)
    dtr = in_ref[0, :, SSD_WIDTH + SSD_XBC:SSD_COLS] + dtb_ref[...]
    dt_scr[...] = jnp.maximum(dtr, 0.0) + jnp.log1p(jnp.exp(-jnp.abs(dtr)))

    row = _iota((T, T), 0)
    col = _iota((T, T), 1)
    tril = row >= col
    tri01 = jnp.where(tril, 1.0, 0.0).astype(BF16)
    lane_lo = col < SSD_HEAD_DIM
    lane_lo_row = _iota((1, LANES), 1) < SSD_HEAD_DIM

    def chunk(c, carry):
        r0 = pl.multiple_of(c * T, T)
        dt_c = dt_scr[pl.ds(r0, T), :]
        a_cum = _dot_exact01(tri01, dt_c * a_ref[...])
        a_last = a_cum[T - 1:T, :]
        ea = jnp.exp(a_cum)
        ea_last = jnp.exp(a_last)
        a_cum_t = a_cum.T
        dt_t = dt_c.T
        w_t = (jnp.exp(a_last - a_cum) * dt_c).T
        for g in range(SSD_GROUPS):
            bg = act_scr[pl.ds(r0, T), SSD_WIDTH + g * SSD_STATE:SSD_WIDTH + (g + 1) * SSD_STATE]
            c0 = SSD_WIDTH + SSD_GROUPS * SSD_STATE + g * SSD_STATE
            cg = act_scr[pl.ds(r0, T), c0:c0 + SSD_STATE]
            scores = _dot_nt(cg.astype(BF16), bg.astype(BF16))
            bg_t = bg.T
            for hp in range(SSD_PAIRS // SSD_GROUPS):
                lt = g * (SSD_PAIRS // SSD_GROUPS) + hp
                x2 = act_scr[pl.ds(r0, T), lt * LANES:(lt + 1) * LANES]
                h_prev = hst[lt]
                ypair = jnp.zeros((T, LANES), F32)
                spair = jnp.zeros((SSD_STATE, LANES), F32)
                for j in range(2):
                    hh = 2 * lt + j
                    lm = lane_lo if j == 0 else jnp.logical_not(lane_lo)
                    xm = jnp.where(lm, x2, 0.0).astype(BF16)
                    hm = jnp.where(lm, h_prev, 0.0).astype(BF16)
                    seg = a_cum[:, hh:hh + 1] - a_cum_t[hh:hh + 1, :]
                    decay = jnp.where(tril, jnp.exp(jnp.minimum(seg, 0.0)), 0.0)
                    w = scores * decay * dt_t[hh:hh + 1, :]
                    ce = cg * ea[:, hh:hh + 1]
                    lhs = jnp.concatenate([w.astype(BF16), ce.astype(BF16)], axis=1)
                    rhs = jnp.concatenate([xm, hm], axis=0)
                    ypair = ypair + _dot(lhs, rhs)
                    spair = spair + _dot((bg_t * w_t[hh:hh + 1, :]).astype(BF16), xm)
                cd = jnp.where(lane_lo_row, ea_last[:, 2 * lt:2 * lt + 1], ea_last[:, 2 * lt + 1:2 * lt + 2])
                hst[lt] = h_prev * cd + spair
                y_buf[pl.ds(r0, T), lt * LANES:(lt + 1) * LANES] = (
                    ypair + dsk_ref[:, lt * LANES:(lt + 1) * LANES] * x2)
        return carry

    lax.fori_loop(0, tt // T, chunk, 0)

    z = in_ref[0, :, 0:SSD_WIDTH]
    y = y_buf[...] * _silu(z)
    gw = SSD_WIDTH // SSD_GROUPS
    parts = []
    for g in range(SSD_GROUPS):
        yg = y[:, g * gw:(g + 1) * gw]
        ms = jnp.mean(yg * yg, axis=-1, keepdims=True)
        parts.append(yg * lax.rsqrt(ms + EPS))
    o_ref[0] = (jnp.concatenate(parts, axis=1) * nw_ref[...]).astype(BF16)


def _ssd(ssd_in, cw, cb, dtb, a, dsk, nw):
    bsz, seqlen, _ = ssd_in.shape
    tt = min(SSD_TT, seqlen)
    const = lambda shape: pl.BlockSpec(shape, lambda b, i: (0,) * len(shape))
    return pl.pallas_call(
        _ssd_kernel,
        grid=(bsz, seqlen // tt),
        in_specs=[
            pl.BlockSpec((1, tt, SSD_COLS), lambda b, i: (b, i, 0)),
            const((SSD_CONV, SSD_XBC)),
            const((1, SSD_XBC)),
            const((1, SSD_DT_PAD)),
            const((1, SSD_DT_PAD)),
            const((1, SSD_WIDTH)),
            const((1, SSD_WIDTH)),
        ],
        out_specs=pl.BlockSpec((1, tt, SSD_WIDTH), lambda b, i: (b, i, 0)),
        out_shape=jax.ShapeDtypeStruct((bsz, seqlen, SSD_WIDTH), BF16),
        scratch_shapes=[
            pltpu.VMEM((tt + SUBLANES, SSD_XBC), F32),
            pltpu.VMEM((tt, SSD_XBC), F32),
            pltpu.VMEM((tt, SSD_DT_PAD), F32),
            pltpu.VMEM((tt, SSD_WIDTH), F32),
            pltpu.VMEM((SSD_PAIRS, SSD_STATE, LANES), F32),
        ],
        compiler_params=pltpu.CompilerParams(
            dimension_semantics=("parallel", "arbitrary"), vmem_limit_bytes=VMEM_LIMIT),
        name="ssd",
    )(ssd_in, cw, cb, dtb, a, dsk, nw)


def _hgrn_kernel(in_ref, loglb_ref, log1mlb_ref, omlb_ref, nw_ref, o_ref,
                 q_scr, k_scr, lc_scr, s_scr, o_buf):
    tt = in_ref.shape[1]
    C = HGRN_CHUNK
    W = HGRN_WIDTH
    nsub = C // SUBLANES

    @pl.when(pl.program_id(1) == 0)
    def _():
        s_scr[...] = jnp.zeros(s_scr.shape, F32)

    qr = in_ref[0, :, 0:W]
    fr = in_ref[0, :, W:2 * W]
    q_scr[...] = _silu(qr)
    log_sig = jnp.minimum(fr, 0.0) - jnp.log1p(jnp.exp(-jnp.abs(fr)))
    t_b = log1mlb_ref[...] + log_sig
    t_a = loglb_ref[...]
    log_f = jnp.maximum(t_a, t_b) + jnp.log1p(jnp.exp(-jnp.abs(t_a - t_b)))
    k_scr[...] = omlb_ref[...] * _sigmoid(-fr)
    lc = _cumsum8(log_f, _iota((tt, W), 0) % SUBLANES)
    for h in range(HGRN_HEADS):
        lc_scr[h] = lc[:, h * LANES:(h + 1) * LANES]

    r8 = _iota((SUBLANES, LANES), 0)
    rr = _iota((C, C), 0)
    cc = _iota((C, C), 1)
    rl = _iota((C, LANES), 0)
    rsel = jnp.where(_iota((SUBLANES * LANES, C), 0) // LANES == _iota((SUBLANES * LANES, C), 1) % SUBLANES,
                     1.0, 0.0).astype(BF16)
    diag_mask = (rr // SUBLANES) == (cc // SUBLANES)

    def bcast_rows(src, idx, reps):
        return jnp.concatenate(
            [jnp.broadcast_to(src[i:i + 1, :], (reps, LANES)) for i in idx], axis=0)

    def chunk(c, carry):
        r0 = pl.multiple_of(c * C, C)
        for h in range(HGRN_HEADS):
            ls = slice(h * LANES, (h + 1) * LANES)
            q = q_scr[pl.ds(r0, C), ls]
            k = k_scr[pl.ds(r0, C), ls]
            v = in_ref[0, pl.ds(r0, C), 2 * W + h * LANES:2 * W + (h + 1) * LANES]
            lc_c = lc_scr[h, pl.ds(r0, C), :]
            tot = lc_scr[h, pl.ds(r0 + SUBLANES - 1, nsub, stride=SUBLANES), :]
            ci = _cumsum8(tot, r8)
            ce = ci - tot
            cum = lc_c + bcast_rows(ce, range(nsub), SUBLANES)
            last = ci[nsub - 1:nsub, :]

            s_t = s_scr[h]
            o_c = _dot_nt((q * jnp.exp(cum)).astype(BF16), s_t.astype(BF16))
            kd = (k * jnp.exp(last - cum)).astype(BF16)
            s_scr[h] = s_t * jnp.exp(last) + _dot_tn(v.astype(BF16), kd)

            q3 = None
            k3 = k.reshape(nsub, SUBLANES, LANES)
            l3 = lc_c.reshape(nsub, SUBLANES, LANES)
            ps = []
            for s in range(SUBLANES):
                ks = jnp.broadcast_to(k3[:, s:s + 1, :], k3.shape).reshape(C, LANES)
                lcs = jnp.broadcast_to(l3[:, s:s + 1, :], l3.shape).reshape(C, LANES)
                e = jnp.exp(jnp.minimum(lc_c - lcs, 0.0))
                ps.append(jnp.where(rl % SUBLANES >= s, q * ks * e, 0.0).astype(BF16))
            attn = jnp.where(diag_mask, _dot(jnp.concatenate(ps, axis=1), rsel), 0.0)
            for m in (8, 16, 32):
                nblk = C // (2 * m)
                cmid = bcast_rows(ci, [(2 * j + 1) * (m // SUBLANES) - 1 for j in range(nblk)], 2 * m)
                second = (rl % (2 * m)) >= m
                qm = jnp.where(second, q * jnp.exp(jnp.minimum(cum - cmid, 0.0)), 0.0).astype(BF16)
                km = jnp.where(second, 0.0, k * jnp.exp(jnp.minimum(cmid - cum, 0.0))).astype(BF16)
                am = _dot_nt(qm, km)
                if nblk > 1:
                    am = jnp.where((rr // (2 * m)) == (cc // (2 * m)), am, 0.0)
                attn = attn + am
            o_c = o_c + _dot(attn.astype(BF16), v.astype(BF16))
            o_buf[pl.ds(r0, C), ls] = o_c
        return carry

    lax.fori_loop(0, tt // C, chunk, 0)

    g = in_ref[0, :, 3 * W:4 * W]
    o = o_buf[...]
    parts = []
    for h in range(HGRN_HEADS):
        oh = o[:, h * LANES:(h + 1) * LANES]
        ms = jnp.mean(oh * oh, axis=-1, keepdims=True)
        parts.append(oh * lax.rsqrt(ms + EPS))
    o_ref[0] = (jnp.concatenate(parts, axis=1) * nw_ref[...] * _silu(g)).astype(BF16)


def _hgrn(hg_in, loglb, log1mlb, omlb, nw):
    bsz, seqlen, _ = hg_in.shape
    tt = min(HGRN_TT, seqlen)
    const = lambda shape: pl.BlockSpec(shape, lambda b, i: (0,) * len(shape))
    return pl.pallas_call(
        _hgrn_kernel,
        grid=(bsz, seqlen // tt),
        in_specs=[
            pl.BlockSpec((1, tt, HGRN_COLS), lambda b, i: (b, i, 0)),
            const((1, HGRN_WIDTH)), const((1, HGRN_WIDTH)), const((1, HGRN_WIDTH)), const((1, HGRN_WIDTH)),
        ],
        out_specs=pl.BlockSpec((1, tt, HGRN_WIDTH), lambda b, i: (b, i, 0)),
        out_shape=jax.ShapeDtypeStruct((bsz, seqlen, HGRN_WIDTH), BF16),
        scratch_shapes=[
            pltpu.VMEM((tt, HGRN_WIDTH), F32),
            pltpu.VMEM((tt, HGRN_WIDTH), F32),
            pltpu.VMEM((HGRN_HEADS, tt, LANES), F32),
            pltpu.VMEM((HGRN_HEADS, HGRN_HEAD_DIM, HGRN_HEAD_DIM), F32),
            pltpu.VMEM((tt, HGRN_WIDTH), F32),
        ],
        compiler_params=pltpu.CompilerParams(
            dimension_semantics=("parallel", "arbitrary"), vmem_limit_bytes=VMEM_LIMIT),
        name="hgrn2",
    )(hg_in, loglb, log1mlb, omlb, nw)


def _s5_kernel(in_ref, bd_ref, are_ref, aim_ref, cw_ref, dsk_ref, gw_ref, gb_ref, nw_ref, o_ref,
               st_scr, bu_scr, x_scr):
    tt, nb, _ = in_ref.shape
    rows = tt * nb
    W = S5_WIDTH

    @pl.when(pl.program_id(0) == 0)
    def _():
        st_scr[...] = jnp.zeros(st_scr.shape, F32)

    u = in_ref[:, :, 0:W].reshape(rows, W)
    ub = u.astype(BF16)
    tiles_per_k = MXU_DIM // (2 * S5_GROUP_SIZE)
    for j in range(S5_TILES):
        kt = j // tiles_per_k
        bu = _dot(ub[:, kt * MXU_DIM:(kt + 1) * MXU_DIM], bd_ref[j])
        bu_scr[:, :, j * MXU_DIM:(j + 1) * MXU_DIM] = bu.reshape(tt, nb, MXU_DIM)

    for s0 in range(0, S5_TILES, S5_STRIP):
        c0 = s0 * MXU_DIM
        sw = S5_STRIP * MXU_DIM
        a_re = [jnp.broadcast_to(are_ref[s0 + jj], (nb, LANES)) for jj in range(S5_STRIP)]
        a_im = [jnp.broadcast_to(aim_ref[s0 + jj], (nb, LANES)) for jj in range(S5_STRIP)]

        def step(t, st, c0=c0, sw=sw, a_re=a_re, a_im=a_im):
            b = bu_scr[t, :, c0:c0 + sw]
            parts = []
            for jj in range(S5_STRIP):
                o_re = jj * MXU_DIM
                o_im = o_re + LANES
                xr, xi = st[:, o_re:o_re + LANES], st[:, o_im:o_im + LANES]
                parts.append(a_re[jj] * xr - a_im[jj] * xi + b[:, o_re:o_re + LANES])
                parts.append(a_re[jj] * xi + a_im[jj] * xr + b[:, o_im:o_im + LANES])
            new = jnp.concatenate(parts, axis=1)
            x_scr[pl.ds(pl.multiple_of(t * nb, nb), nb), c0:c0 + sw] = new.astype(BF16)
            return new

        st_scr[:, c0:c0 + sw] = lax.fori_loop(0, tt, step, st_scr[:, c0:c0 + sw])

    ys = []
    tiles_per_n = S5_TILES // (W // MXU_DIM)
    for o in range(W // MXU_DIM):
        acc = jnp.zeros((rows, MXU_DIM), F32)
        for j in range(o * tiles_per_n, (o + 1) * tiles_per_n):
            acc = acc + _dot(x_scr[:, j * MXU_DIM:(j + 1) * MXU_DIM], cw_ref[j])
        ys.append(acc)
    y = jnp.concatenate(ys, axis=1) + dsk_ref[...] * u
    y = 0.5 * y * (1.0 + jnp.tanh(0.7978845608028654 * (y + 0.044715 * (y * y * y))))
    hg = _dot(y.astype(BF16), gw_ref[...]) + gb_ref[...]
    gate = in_ref[:, :, W:2 * W].reshape(rows, W)
    yy = hg[:, 0:W] * _sigmoid(hg[:, W:2 * W]) * _silu(gate)
    ms = jnp.mean(yy * yy, axis=-1, keepdims=True)
    o_ref[...] = (yy * lax.rsqrt(ms + EPS) * nw_ref[...]).reshape(tt, nb, W)


def _s5(s5_in, bd, are, aim, cw, dsk, gw, gb, nw):
    seqlen, bsz, _ = s5_in.shape
    tt = min(S5_TT, seqlen)
    const = lambda shape: pl.BlockSpec(shape, lambda i: (0,) * len(shape))
    return pl.pallas_call(
        _s5_kernel,
        grid=(seqlen // tt,),
        in_specs=[
            pl.BlockSpec((tt, bsz, S5_COLS), lambda i: (i, 0, 0)),
            const((S5_TILES, MXU_DIM, MXU_DIM)),
            const((S5_TILES, 1, LANES)),
            const((S5_TILES, 1, LANES)),
            const((S5_TILES, MXU_DIM, MXU_DIM)),
            const((1, S5_WIDTH)),
            const((S5_WIDTH, 2 * S5_WIDTH)),
            const((1, 2 * S5_WIDTH)),
            const((1, S5_WIDTH)),
        ],
        out_specs=pl.BlockSpec((tt, bsz, S5_WIDTH), lambda i: (i, 0, 0)),
        out_shape=jax.ShapeDtypeStruct((seqlen, bsz, S5_WIDTH), F32),
        scratch_shapes=[
            pltpu.VMEM((bsz, S5_NSTATE), F32),
            pltpu.VMEM((tt, bsz, S5_NSTATE), F32),
            pltpu.VMEM((tt * bsz, S5_NSTATE), BF16),
        ],
        compiler_params=pltpu.CompilerParams(
            dimension_semantics=("arbitrary",), vmem_limit_bytes=VMEM_LIMIT),
        name="s5",
    )(s5_in, bd, are, aim, cw, dsk, gw, gb, nw)


def _s5_params(lam_re, lam_im, log_step, b_re, b_im, c_re, c_im):
    lr = jnp.minimum(lam_re.astype(F32), -S5_MIN_NEG)
    li = lam_im.astype(F32)
    step = jnp.exp(log_step.astype(F32))[:, None]
    mag = jnp.exp(lr * step)
    ang = li * step
    ab_re, ab_im = mag * jnp.cos(ang), mag * jnp.sin(ang)
    den = lr * lr + li * li
    num_re, num_im = ab_re - 1.0, ab_im
    co_re = (num_re * lr + num_im * li) / den
    co_im = (num_im * lr - num_re * li) / den
    bb_re = co_re[..., None] * b_re - co_im[..., None] * b_im
    bb_im = co_re[..., None] * b_im + co_im[..., None] * b_re
    G, P, H = S5_GROUPS, S5_STATE, S5_GROUP_SIZE
    pair = lambda t: t.reshape(S5_TILES, 2, *t.shape[1:])
    eye2 = jnp.eye(2, dtype=F32)
    blk_re = jnp.einsum("jgph,gk->jghkp", pair(bb_re), eye2).reshape(S5_TILES, 2 * H, 2 * P)
    blk_im = jnp.einsum("jgph,gk->jghkp", pair(bb_im), eye2).reshape(S5_TILES, 2 * H, 2 * P)
    blk = jnp.concatenate([blk_re, blk_im], axis=-1)
    tiles_per_k = MXU_DIM // (2 * H)
    rows = (jnp.arange(S5_TILES) % tiles_per_k)[:, None] * (2 * H) + jnp.arange(2 * H)[None, :]
    bd = jnp.zeros((S5_TILES, MXU_DIM, MXU_DIM), F32)
    bd = bd.at[jnp.arange(S5_TILES)[:, None], rows].set(blk)
    out_re = jnp.einsum("jghp,gk->jgpkh", pair(c_re.astype(F32)), eye2).reshape(S5_TILES, 2 * P, 2 * H)
    out_im = jnp.einsum("jghp,gk->jgpkh", pair(c_im.astype(F32)), eye2).reshape(S5_TILES, 2 * P, 2 * H)
    oblk = jnp.concatenate([out_re, -out_im], axis=1)
    cw = jnp.zeros((S5_TILES, MXU_DIM, MXU_DIM), F32)
    cw = cw.at[jnp.arange(S5_TILES)[:, None], :, rows].set(jnp.swapaxes(oblk, 1, 2))
    are = pair(ab_re).reshape(S5_TILES, 1, 2 * P)
    aim = pair(ab_im).reshape(S5_TILES, 1, 2 * P)
    return bd.astype(BF16), are, aim, cw.astype(BF16)


def _layer(x, p):
    ssd_in, hg_in, s5_in = _inproj(x, p["pre_w"], p["w_ssd"], p["w_hg"], p["w_s5"])
    y_ssd = _ssd(ssd_in, p["conv_w"], p["conv_b"], p["dt_bias"], p["a"], p["ssd_d"], p["ssd_nw"])
    y_hg = _hgrn(hg_in, p["log_lb"], p["log1m_lb"], p["om_lb"], p["hg_nw"])
    y_s5 = _s5(s5_in, p["bd"], p["are"], p["aim"], p["cw"], p["s5_d"], p["glu_w"], p["glu_b"], p["s5_nw"])
    return _outproj(x, y_ssd, y_hg, y_s5, p["w_out"], p["post_w"])


def _prep_layer(l, lb_all, pre_norm_w, post_norm_w, w_in, w_out, ssd_conv_w, ssd_conv_b, ssd_dt_bias,
                ssd_a_log, ssd_d, ssd_norm_w, hgrn_norm_w, s5_lambda_re, s5_lambda_im, s5_log_step,
                s5_b_re, s5_b_im, s5_c_re, s5_c_im, s5_d, s5_glu_w, s5_glu_b, s5_norm_w):
    row = lambda v: v.astype(F32).reshape(1, -1)
    pad_heads = lambda v: jnp.pad(v.astype(F32), (0, SSD_DT_PAD - SSD_HEADS)).reshape(1, -1)
    w = w_in[l]
    o_dt = SSD_WIDTH + SSD_XBC
    o_hg = o_dt + SSD_HEADS
    o_s5 = o_hg + HGRN_COLS
    w_ssd = jnp.concatenate(
        [w[:, :o_dt], jnp.pad(w[:, o_dt:o_hg], ((0, 0), (0, SSD_DT_PAD - SSD_HEADS)))], axis=1)
    lb = lb_all[l]
    bd, are, aim, cw = _s5_params(s5_lambda_re[l], s5_lambda_im[l], s5_log_step[l],
                                  s5_b_re[l], s5_b_im[l], s5_c_re[l], s5_c_im[l])
    return dict(
        pre_w=row(pre_norm_w[l]), post_w=row(post_norm_w[l]),
        w_ssd=w_ssd.astype(BF16), w_hg=w[:, o_hg:o_s5].astype(BF16), w_s5=w[:, o_s5:].astype(BF16),
        w_out=w_out[l].astype(BF16),
        conv_w=ssd_conv_w[l].astype(F32), conv_b=row(ssd_conv_b[l]),
        dt_bias=pad_heads(ssd_dt_bias[l]), a=pad_heads(-jnp.exp(ssd_a_log[l].astype(F32))),
        ssd_d=row(jnp.repeat(ssd_d[l].astype(F32), SSD_HEAD_DIM)), ssd_nw=row(ssd_norm_w[l]),
        log_lb=row(jnp.log(lb)), log1m_lb=row(jnp.log1p(-lb)), om_lb=row(1.0 - lb), hg_nw=row(hgrn_norm_w[l]),
        bd=bd, are=are, aim=aim, cw=cw, s5_d=row(s5_d[l]), glu_w=s5_glu_w[l].astype(BF16),
        glu_b=row(s5_glu_b[l]), s5_nw=row(s5_norm_w[l]),
    )


def kernel(x, pre_norm_w, post_norm_w, w_in, w_out, ssd_conv_w, ssd_conv_b, ssd_dt_bias, ssd_a_log, ssd_d, ssd_norm_w, hgrn_lower_bounds, hgrn_norm_w, s5_lambda_re, s5_lambda_im, s5_log_step, s5_b_re, s5_b_im, s5_c_re, s5_c_im, s5_d, s5_glu_w, s5_glu_b, s5_norm_w):
    lb_all = jnp.cumsum(jax.nn.softmax(hgrn_lower_bounds.astype(F32), axis=0), axis=0)
    lb_all = lb_all - lb_all[0:1]
    for l in range(w_in.shape[0]):
        p = _prep_layer(l, lb_all, pre_norm_w, post_norm_w, w_in, w_out, ssd_conv_w, ssd_conv_b,
                        ssd_dt_bias, ssd_a_log, ssd_d, ssd_norm_w, hgrn_norm_w, s5_lambda_re,
                        s5_lambda_im, s5_log_step, s5_b_re, s5_b_im, s5_c_re, s5_c_im, s5_d,
                        s5_glu_w, s5_glu_b, s5_norm_w)
        x = _layer(x, p)
    return x
```

```python
import jax
import jax.numpy as jnp
from jax import lax
from jax.experimental import pallas as pl
from jax.experimental.pallas import tpu as pltpu

F32 = jnp.float32
BF16 = jnp.bfloat16
EPS = 1e-6

D_MODEL = 1024
LANES = 128
SUBLANES = 8
MXU_DIM = 256

SSD_WIDTH = 1024
SSD_HEADS = 16
SSD_HEAD_DIM = 64
SSD_GROUPS = 2
SSD_STATE = 128
SSD_CHUNK = 128
SSD_CONV = 4
SSD_XBC = SSD_WIDTH + 2 * SSD_GROUPS * SSD_STATE
SSD_DT_PAD = LANES
SSD_COLS = SSD_WIDTH + SSD_XBC + SSD_DT_PAD
SSD_PAIRS = SSD_HEADS // 2

HGRN_WIDTH = 512
HGRN_HEADS = 4
HGRN_HEAD_DIM = 128
HGRN_CHUNK = 64
HGRN_COLS = 4 * HGRN_WIDTH
MIX_COLS = SSD_COLS + HGRN_COLS

S5_WIDTH = 512
S5_GROUPS = 32
S5_GROUP_SIZE = 16
S5_STATE = 64
S5_MIN_NEG = 1e-4
S5_COLS = 2 * S5_WIDTH
S5_TILES = S5_GROUPS // 2
S5_NSTATE = S5_TILES * MXU_DIM
S5_STRIP = 4

MIX_TT = 256
S5_TT = 32
VMEM_LIMIT = 56 * 1024 * 1024


def _sigmoid(x):
    return 0.5 * jnp.tanh(0.5 * x) + 0.5


def _silu(x):
    hx = 0.5 * x
    return hx * jnp.tanh(hx) + hx


def _dot(a, b):
    return jnp.dot(a, b, preferred_element_type=F32)


def _dot_nt(a, b):
    return lax.dot_general(a, b, (((1,), (1,)), ((), ())), preferred_element_type=F32)


def _dot_tn(a, b):
    return lax.dot_general(a, b, (((0,), (0,)), ((), ())), preferred_element_type=F32)


def _dot_exact01(m01, x):
    hi = x.astype(BF16)
    r1 = x - hi.astype(F32)
    mid = r1.astype(BF16)
    lo = (r1 - mid.astype(F32)).astype(BF16)
    return _dot(m01, hi) + _dot(m01, mid) + _dot(m01, lo)


def _iota(shape, axis):
    return lax.broadcasted_iota(jnp.int32, shape, axis)


def _cumsum8(x, rows_mod8):
    for sh in (1, 2, 4):
        x = x + jnp.where(rows_mod8 >= sh, pltpu.roll(x, sh, axis=0), 0.0)
    return x


def _rms_normed(x, w):
    ms = jnp.mean(x * x, axis=-1, keepdims=True)
    return x * lax.rsqrt(ms + EPS) * w


def _s5_kernel(x_ref, pw_ref, w_ref, bd_ref, are_ref, aim_ref, cw_ref, dsk_ref, gw_ref, gb_ref, nw_ref,
               o_ref, st_scr, in_scr, bu_scr, x_scr):
    nb, tt, d = x_ref.shape
    rows = tt * nb
    W = S5_WIDTH

    @pl.when(pl.program_id(0) == 0)
    def _():
        st_scr[...] = jnp.zeros(st_scr.shape, F32)

    h = _rms_normed(x_ref[...].reshape(rows, d), pw_ref[...]).astype(BF16)
    proj = _dot(h, w_ref[...])
    for b in range(nb):
        in_scr[:, b, :] = proj[b * tt:(b + 1) * tt, :]

    u = in_scr[:, :, 0:W].reshape(rows, W)
    ub = u.astype(BF16)
    tiles_per_k = MXU_DIM // (2 * S5_GROUP_SIZE)
    for j in range(S5_TILES):
        kt = j // tiles_per_k
        bu = _dot(ub[:, kt * MXU_DIM:(kt + 1) * MXU_DIM], bd_ref[j])
        bu_scr[:, :, j * MXU_DIM:(j + 1) * MXU_DIM] = bu.reshape(tt, nb, MXU_DIM)

    for s0 in range(0, S5_TILES, S5_STRIP):
        c0 = s0 * MXU_DIM
        sw = S5_STRIP * MXU_DIM
        a_re = [jnp.broadcast_to(are_ref[s0 + jj], (nb, LANES)) for jj in range(S5_STRIP)]
        a_im = [jnp.broadcast_to(aim_ref[s0 + jj], (nb, LANES)) for jj in range(S5_STRIP)]

        def step(t, st, c0=c0, sw=sw, a_re=a_re, a_im=a_im):
            b = bu_scr[t, :, c0:c0 + sw]
            parts = []
            for jj in range(S5_STRIP):
                o_re = jj * MXU_DIM
                o_im = o_re + LANES
                xr, xi = st[:, o_re:o_re + LANES], st[:, o_im:o_im + LANES]
                parts.append(a_re[jj] * xr - a_im[jj] * xi + b[:, o_re:o_re + LANES])
                parts.append(a_re[jj] * xi + a_im[jj] * xr + b[:, o_im:o_im + LANES])
            new = jnp.concatenate(parts, axis=1)
            x_scr[pl.ds(pl.multiple_of(t * nb, nb), nb), c0:c0 + sw] = new.astype(BF16)
            return new

        st_scr[:, c0:c0 + sw] = lax.fori_loop(0, tt, step, st_scr[:, c0:c0 + sw], unroll=True)

    ys = []
    tiles_per_n = S5_TILES // (W // MXU_DIM)
    for o in range(W // MXU_DIM):
        acc = jnp.zeros((rows, MXU_DIM), F32)
        for j in range(o * tiles_per_n, (o + 1) * tiles_per_n):
            acc = acc + _dot(x_scr[:, j * MXU_DIM:(j + 1) * MXU_DIM], cw_ref[j])
        ys.append(acc)
    y = jnp.concatenate(ys, axis=1) + dsk_ref[...] * u
    y = 0.5 * y * (1.0 + jnp.tanh(0.7978845608028654 * (y + 0.044715 * (y * y * y))))
    hg = _dot(y.astype(BF16), gw_ref[...]) + gb_ref[...]
    gate = in_scr[:, :, W:2 * W].reshape(rows, W)
    yy = hg[:, 0:W] * _sigmoid(hg[:, W:2 * W]) * _silu(gate)
    out = _rms_normed(yy, nw_ref[...]).reshape(tt, nb, W)
    for t in range(tt):
        o_ref[:, t, :] = out[t]


def _s5(x, pw, w, bd, are, aim, cw, dsk, gw, gb, nw):
    bsz, seqlen, d = x.shape
    tt = min(S5_TT, seqlen)
    const = lambda shape: pl.BlockSpec(shape, lambda i: (0,) * len(shape), pipeline_mode=pl.Buffered(1))
    return pl.pallas_call(
        _s5_kernel,
        grid=(seqlen // tt,),
        in_specs=[
            pl.BlockSpec((bsz, tt, d), lambda i: (0, i, 0)),
            const((1, d)),
            const((d, S5_COLS)),
            const((S5_TILES, MXU_DIM, MXU_DIM)),
            const((S5_TILES, 1, LANES)),
            const((S5_TILES, 1, LANES)),
            const((S5_TILES, MXU_DIM, MXU_DIM)),
            const((1, S5_WIDTH)),
            const((S5_WIDTH, 2 * S5_WIDTH)),
            const((1, 2 * S5_WIDTH)),
            const((1, S5_WIDTH)),
        ],
        out_specs=pl.BlockSpec((bsz, tt, S5_WIDTH), lambda i: (0, i, 0)),
        out_shape=jax.ShapeDtypeStruct((bsz, seqlen, S5_WIDTH), F32),
        scratch_shapes=[
            pltpu.VMEM((bsz, S5_NSTATE), F32),
            pltpu.VMEM((tt, bsz, S5_COLS), F32),
            pltpu.VMEM((tt, bsz, S5_NSTATE), F32),
            pltpu.VMEM((tt * bsz, S5_NSTATE), BF16),
        ],
        compiler_params=pltpu.CompilerParams(
            dimension_semantics=("arbitrary",), vmem_limit_bytes=VMEM_LIMIT),
        name="s5",
    )(x, pw, w, bd, are, aim, cw, dsk, gw, gb, nw)


def _ssd_prologue(p_cur, cw_ref, cb_ref, dtb_ref, xbc_buf, act_scr, dt_scr, zs_scr):
    tt = act_scr.shape[0]
    zs_scr[...] = _silu(p_cur[:, 0:SSD_WIDTH])
    xbc_buf[SUBLANES:SUBLANES + tt, :] = p_cur[:, SSD_WIDTH:SSD_WIDTH + SSD_XBC]
    conv = cb_ref[...]
    for k in range(SSD_CONV):
        off = SUBLANES - (SSD_CONV - 1) + k
        conv = conv + cw_ref[k:k + 1, :] * xbc_buf[off:off + tt, :]
    xbc_buf[0:SUBLANES, :] = xbc_buf[tt:tt + SUBLANES, :]
    act_scr[...] = _silu(conv)
    dtr = p_cur[:, SSD_WIDTH + SSD_XBC:SSD_COLS] + dtb_ref[...]
    dt_scr[...] = jnp.maximum(dtr, 0.0) + jnp.log1p(jnp.exp(-jnp.abs(dtr)))


def _ssd_chunks(a_ref, dsk_ref, act_scr, dt_scr, y_buf, hst):
    tt = act_scr.shape[0]
    T = SSD_CHUNK
    row = _iota((T, T), 0)
    col = _iota((T, T), 1)
    tril = row >= col
    tri01 = jnp.where(tril, 1.0, 0.0).astype(BF16)
    lane_lo = col < SSD_HEAD_DIM
    lane_lo_row = _iota((1, LANES), 1) < SSD_HEAD_DIM
    pairs_per_group = SSD_PAIRS // SSD_GROUPS

    def chunk(c, carry):
        r0 = pl.multiple_of(c * T, T)
        dt_c = dt_scr[pl.ds(r0, T), :]
        a_cum = _dot_exact01(tri01, dt_c * a_ref[...])
        a_last = a_cum[T - 1:T, :]
        ea = jnp.exp(a_cum)
        ea_last = jnp.exp(a_last)
        a_cum_t = a_cum.T
        dt_t = dt_c.T
        w_t = (jnp.exp(a_last - a_cum) * dt_c).T
        for g in range(SSD_GROUPS):
            bg = act_scr[pl.ds(r0, T), SSD_WIDTH + g * SSD_STATE:SSD_WIDTH + (g + 1) * SSD_STATE]
            c0 = SSD_WIDTH + SSD_GROUPS * SSD_STATE + g * SSD_STATE
            cg = act_scr[pl.ds(r0, T), c0:c0 + SSD_STATE]
            scores = _dot_nt(cg.astype(BF16), bg.astype(BF16))
            bg_t = bg.T
            for hp in range(pairs_per_group):
                lt = g * pairs_per_group + hp
                x2 = act_scr[pl.ds(r0, T), lt * LANES:(lt + 1) * LANES]
                h_prev = hst[lt]
                ypair = jnp.zeros((T, LANES), F32)
                spair = jnp.zeros((SSD_STATE, LANES), F32)
                for j in range(2):
                    hh = 2 * lt + j
                    lm = lane_lo if j == 0 else jnp.logical_not(lane_lo)
                    xm = jnp.where(lm, x2, 0.0).astype(BF16)
                    hm = jnp.where(lm, h_prev, 0.0).astype(BF16)
                    seg = a_cum[:, hh:hh + 1] - a_cum_t[hh:hh + 1, :]
                    decay = jnp.where(tril, jnp.exp(jnp.minimum(seg, 0.0)), 0.0)
                    w = scores * decay * dt_t[hh:hh + 1, :]
                    ce = cg * ea[:, hh:hh + 1]
                    lhs = jnp.concatenate([w.astype(BF16), ce.astype(BF16)], axis=1)
                    rhs = jnp.concatenate([xm, hm], axis=0)
                    ypair = ypair + _dot(lhs, rhs)
                    spair = spair + _dot((bg_t * w_t[hh:hh + 1, :]).astype(BF16), xm)
                cd = jnp.where(lane_lo_row, ea_last[:, 2 * lt:2 * lt + 1], ea_last[:, 2 * lt + 1:2 * lt + 2])
                hst[lt] = h_prev * cd + spair
                y_buf[pl.ds(r0, T), lt * LANES:(lt + 1) * LANES] = (
                    ypair + dsk_ref[:, lt * LANES:(lt + 1) * LANES] * x2)
        return carry

    lax.fori_loop(0, tt // T, chunk, 0, unroll=True)


def _ssd_epilogue(nw_ref, y_buf, zs_scr):
    y = y_buf[...] * zs_scr[...]
    gw = SSD_WIDTH // SSD_GROUPS
    parts = []
    for g in range(SSD_GROUPS):
        yg = y[:, g * gw:(g + 1) * gw]
        ms = jnp.mean(yg * yg, axis=-1, keepdims=True)
        parts.append(yg * lax.rsqrt(ms + EPS))
    return (jnp.concatenate(parts, axis=1) * nw_ref[...]).astype(BF16)


def _hgrn_prologue(p_cur, loglb_ref, log1mlb_ref, omlb_ref, q_scr, k_scr, lc_scr, v_scr, gs_scr):
    tt = q_scr.shape[0]
    W = HGRN_WIDTH
    v_scr[...] = p_cur[:, SSD_COLS + 2 * W:SSD_COLS + 3 * W].astype(BF16)
    gs_scr[...] = _silu(p_cur[:, SSD_COLS + 3 * W:SSD_COLS + 4 * W])
    qr = p_cur[:, SSD_COLS:SSD_COLS + W]
    fr = p_cur[:, SSD_COLS + W:SSD_COLS + 2 * W]
    q_scr[...] = _silu(qr)
    log_sig = jnp.minimum(fr, 0.0) - jnp.log1p(jnp.exp(-jnp.abs(fr)))
    t_b = log1mlb_ref[...] + log_sig
    t_a = loglb_ref[...]
    log_f = jnp.maximum(t_a, t_b) + jnp.log1p(jnp.exp(-jnp.abs(t_a - t_b)))
    k_scr[...] = omlb_ref[...] * _sigmoid(-fr)
    lc = _cumsum8(log_f, _iota((tt, W), 0) % SUBLANES)
    for h in range(HGRN_HEADS):
        lc_scr[h] = lc[:, h * LANES:(h + 1) * LANES]


def _hgrn_chunks(v_scr, q_scr, k_scr, lc_scr, s_scr, o_buf):
    tt = q_scr.shape[0]
    C = HGRN_CHUNK
    nsub = C // SUBLANES
    r8 = _iota((SUBLANES, LANES), 0)
    rr = _iota((C, C), 0)
    cc = _iota((C, C), 1)
    rl = _iota((C, LANES), 0)
    rsel = jnp.where(_iota((SUBLANES * LANES, C), 0) // LANES == _iota((SUBLANES * LANES, C), 1) % SUBLANES,
                     1.0, 0.0).astype(BF16)
    diag_mask = ((rr // SUBLANES) == (cc // SUBLANES)) & (rr >= cc)

    def bcast_rows(src, idx, reps):
        return jnp.concatenate(
            [jnp.broadcast_to(src[i:i + 1, :], (reps, LANES)) for i in idx], axis=0)

    def chunk(c, carry):
        r0 = pl.multiple_of(c * C, C)
        for h in range(HGRN_HEADS):
            ls = slice(h * LANES, (h + 1) * LANES)
            q = q_scr[pl.ds(r0, C), ls]
            k = k_scr[pl.ds(r0, C), ls]
            v = v_scr[pl.ds(r0, C), ls]
            lc_c = lc_scr[h, pl.ds(r0, C), :]
            tot = lc_scr[h, pl.ds(r0 + SUBLANES - 1, nsub, stride=SUBLANES), :]
            ci = _cumsum8(tot, r8)
            ce = ci - tot
            cum = lc_c + bcast_rows(ce, range(nsub), SUBLANES)
            last = ci[nsub - 1:nsub, :]

            s_t = s_scr[h]
            o_c = _dot_nt((q * jnp.exp(cum)).astype(BF16), s_t.astype(BF16))
            kd = (k * jnp.exp(last - cum)).astype(BF16)
            s_scr[h] = s_t * jnp.exp(last) + _dot_tn(v, kd)

            k3 = k.reshape(nsub, SUBLANES, LANES)
            l3 = lc_c.reshape(nsub, SUBLANES, LANES)
            ps = []
            for s in range(SUBLANES):
                ks = jnp.broadcast_to(k3[:, s:s + 1, :], k3.shape).reshape(C, LANES)
                lcs = jnp.broadcast_to(l3[:, s:s + 1, :], l3.shape).reshape(C, LANES)
                ps.append((q * ks * jnp.exp(jnp.minimum(lc_c - lcs, 0.0))).astype(BF16))
            attn = jnp.where(diag_mask, _dot(jnp.concatenate(ps, axis=1), rsel), 0.0)
            for m in (8, 16, 32):
                nblk = C // (2 * m)
                cmid = bcast_rows(ci, [(2 * j + 1) * (m // SUBLANES) - 1 for j in range(nblk)], 2 * m)
                second = (rl % (2 * m)) >= m
                qm = jnp.where(second, q * jnp.exp(jnp.minimum(cum - cmid, 0.0)), 0.0).astype(BF16)
                km = jnp.where(second, 0.0, k * jnp.exp(jnp.minimum(cmid - cum, 0.0))).astype(BF16)
                am = _dot_nt(qm, km)
                if nblk > 1:
                    am = jnp.where((rr // (2 * m)) == (cc // (2 * m)), am, 0.0)
                attn = attn + am
            o_buf[pl.ds(r0, C), ls] = o_c + _dot(attn.astype(BF16), v)
        return carry

    lax.fori_loop(0, tt // C, chunk, 0, unroll=True)


def _hgrn_epilogue(nw_ref, o_buf, gs_scr):
    o = o_buf[...]
    parts = []
    for h in range(HGRN_HEADS):
        oh = o[:, h * LANES:(h + 1) * LANES]
        ms = jnp.mean(oh * oh, axis=-1, keepdims=True)
        parts.append(oh * lax.rsqrt(ms + EPS))
    return (jnp.concatenate(parts, axis=1) * nw_ref[...] * gs_scr[...]).astype(BF16)


def _mix_kernel(x_ref, xn_ref, ys5_ref, pw_ref, wssd_ref, whg_ref,
                cw_ref, cb_ref, dtb_ref, a_ref, dsk_ref, snw_ref,
                loglb_ref, log1mlb_ref, omlb_ref, hnw_ref, wo_ref, qw_ref, o_ref,
                p_scr, xbc_buf, act_scr, dt_scr, y_buf, hst, q_scr, k_scr, lc_scr, s_scr, o_buf,
                zs_scr, v_scr, gs_scr):
    i = pl.program_id(1)
    cur = i % 2

    def project(src_ref, dst):
        h = _rms_normed(src_ref[0], pw_ref[...]).astype(BF16)
        dst[:, 0:SSD_COLS] = _dot(h, wssd_ref[...])
        dst[:, SSD_COLS:MIX_COLS] = _dot(h, whg_ref[...])

    @pl.when(i == 0)
    def _():
        xbc_buf[0:SUBLANES, :] = jnp.zeros((SUBLANES, SSD_XBC), F32)
        hst[...] = jnp.zeros(hst.shape, F32)
        s_scr[...] = jnp.zeros(s_scr.shape, F32)
        project(x_ref, p_scr.at[0])

    p_cur = p_scr.at[cur]
    _ssd_prologue(p_cur, cw_ref, cb_ref, dtb_ref, xbc_buf, act_scr, dt_scr, zs_scr)
    _hgrn_prologue(p_cur, loglb_ref, log1mlb_ref, omlb_ref, q_scr, k_scr, lc_scr, v_scr, gs_scr)
    _ssd_chunks(a_ref, dsk_ref, act_scr, dt_scr, y_buf, hst)
    _hgrn_chunks(v_scr, q_scr, k_scr, lc_scr, s_scr, o_buf)
    y_ssd = _ssd_epilogue(snw_ref, y_buf, zs_scr)
    y_hg = _hgrn_epilogue(hnw_ref, o_buf, gs_scr)
    m = (_dot(y_ssd, wo_ref[0:SSD_WIDTH, :])
         + _dot(y_hg, wo_ref[SSD_WIDTH:SSD_WIDTH + HGRN_WIDTH, :])
         + _dot(ys5_ref[0].astype(BF16), wo_ref[SSD_WIDTH + HGRN_WIDTH:, :]))
    o_ref[0] = x_ref[0] + _rms_normed(m, qw_ref[...])
    project(xn_ref, p_scr.at[1 - cur])


def _mix(x, ys5, p):
    bsz, seqlen, d = x.shape
    tt = min(MIX_TT, seqlen)
    nt = seqlen // tt
    const = lambda shape: pl.BlockSpec(shape, lambda b, i: (0,) * len(shape), pipeline_mode=pl.Buffered(1))
    return pl.pallas_call(
        _mix_kernel,
        grid=(bsz, nt),
        in_specs=[
            pl.BlockSpec((1, tt, d), lambda b, i: (b, i, 0)),
            pl.BlockSpec((1, tt, d), lambda b, i: (b, jnp.minimum(i + 1, nt - 1), 0)),
            pl.BlockSpec((1, tt, S5_WIDTH), lambda b, i: (b, i, 0)),
            const((1, d)),
            const((d, SSD_COLS)),
            const((d, HGRN_COLS)),
            const((SSD_CONV, SSD_XBC)),
            const((1, SSD_XBC)),
            const((1, SSD_DT_PAD)),
            const((1, SSD_DT_PAD)),
            const((1, SSD_WIDTH)),
            const((1, SSD_WIDTH)),
            const((1, HGRN_WIDTH)), const((1, HGRN_WIDTH)), const((1, HGRN_WIDTH)), const((1, HGRN_WIDTH)),
            const((2 * d, d)),
            const((1, d)),
        ],
        out_specs=pl.BlockSpec((1, tt, d), lambda b, i: (b, i, 0)),
        out_shape=jax.ShapeDtypeStruct((bsz, seqlen, d), F32),
        scratch_shapes=[
            pltpu.VMEM((2, tt, MIX_COLS), F32),
            pltpu.VMEM((tt + SUBLANES, SSD_XBC), F32),
            pltpu.VMEM((tt, SSD_XBC), F32),
            pltpu.VMEM((tt, SSD_DT_PAD), F32),
            pltpu.VMEM((tt, SSD_WIDTH), F32),
            pltpu.VMEM((SSD_PAIRS, SSD_STATE, LANES), F32),
            pltpu.VMEM((tt, HGRN_WIDTH), F32),
            pltpu.VMEM((tt, HGRN_WIDTH), F32),
            pltpu.VMEM((HGRN_HEADS, tt, LANES), F32),
            pltpu.VMEM((HGRN_HEADS, HGRN_HEAD_DIM, HGRN_HEAD_DIM), F32),
            pltpu.VMEM((tt, HGRN_WIDTH), F32),
            pltpu.VMEM((tt, SSD_WIDTH), F32),
            pltpu.VMEM((tt, HGRN_WIDTH), BF16),
            pltpu.VMEM((tt, HGRN_WIDTH), F32),
        ],
        compiler_params=pltpu.CompilerParams(
            dimension_semantics=("parallel", "arbitrary"), vmem_limit_bytes=VMEM_LIMIT),
        name="mix",
    )(x, x, ys5, p["pre_w"], p["w_ssd"], p["w_hg"],
      p["conv_w"], p["conv_b"], p["dt_bias"], p["a"], p["ssd_d"], p["ssd_nw"],
      p["log_lb"], p["log1m_lb"], p["om_lb"], p["hg_nw"], p["w_out"], p["post_w"])


def _s5_params(lam_re, lam_im, log_step, b_re, b_im, c_re, c_im):
    lr = jnp.minimum(lam_re.astype(F32), -S5_MIN_NEG)
    li = lam_im.astype(F32)
    step = jnp.exp(log_step.astype(F32))[:, None]
    mag = jnp.exp(lr * step)
    ang = li * step
    ab_re, ab_im = mag * jnp.cos(ang), mag * jnp.sin(ang)
    den = lr * lr + li * li
    num_re, num_im = ab_re - 1.0, ab_im
    co_re = (num_re * lr + num_im * li) / den
    co_im = (num_im * lr - num_re * li) / den
    bb_re = co_re[..., None] * b_re - co_im[..., None] * b_im
    bb_im = co_re[..., None] * b_im + co_im[..., None] * b_re
    P, H = S5_STATE, S5_GROUP_SIZE
    pair = lambda t: t.reshape(S5_TILES, 2, *t.shape[1:])
    eye2 = jnp.eye(2, dtype=F32)
    blk_re = jnp.einsum("jgph,gk->jghkp", pair(bb_re), eye2).reshape(S5_TILES, 2 * H, 2 * P)
    blk_im = jnp.einsum("jgph,gk->jghkp", pair(bb_im), eye2).reshape(S5_TILES, 2 * H, 2 * P)
    blk = jnp.concatenate([blk_re, blk_im], axis=-1)
    tiles_per_k = MXU_DIM // (2 * H)
    rows = (jnp.arange(S5_TILES) % tiles_per_k)[:, None] * (2 * H) + jnp.arange(2 * H)[None, :]
    bd = jnp.zeros((S5_TILES, MXU_DIM, MXU_DIM), F32)
    bd = bd.at[jnp.arange(S5_TILES)[:, None], rows].set(blk)
    out_re = jnp.einsum("jghp,gk->jgpkh", pair(c_re.astype(F32)), eye2).reshape(S5_TILES, 2 * P, 2 * H)
    out_im = jnp.einsum("jghp,gk->jgpkh", pair(c_im.astype(F32)), eye2).reshape(S5_TILES, 2 * P, 2 * H)
    oblk = jnp.concatenate([out_re, -out_im], axis=1)
    cw = jnp.zeros((S5_TILES, MXU_DIM, MXU_DIM), F32)
    cw = cw.at[jnp.arange(S5_TILES)[:, None], :, rows].set(jnp.swapaxes(oblk, 1, 2))
    are = pair(ab_re).reshape(S5_TILES, 1, 2 * P)
    aim = pair(ab_im).reshape(S5_TILES, 1, 2 * P)
    return bd.astype(BF16), are, aim, cw.astype(BF16)


def _layer(x, p):
    y_s5 = _s5(x, p["pre_w"], p["w_s5"], p["bd"], p["are"], p["aim"], p["cw"], p["s5_d"],
               p["glu_w"], p["glu_b"], p["s5_nw"])
    return _mix(x, y_s5, p)


def _prep_layer(l, lb_all, pre_norm_w, post_norm_w, w_in, w_out, ssd_conv_w, ssd_conv_b, ssd_dt_bias,
                ssd_a_log, ssd_d, ssd_norm_w, hgrn_norm_w, s5_lambda_re, s5_lambda_im, s5_log_step,
                s5_b_re, s5_b_im, s5_c_re, s5_c_im, s5_d, s5_glu_w, s5_glu_b, s5_norm_w):
    row = lambda v: v.astype(F32).reshape(1, -1)
    pad_heads = lambda v: jnp.pad(v.astype(F32), (0, SSD_DT_PAD - SSD_HEADS)).reshape(1, -1)
    w = w_in[l]
    o_dt = SSD_WIDTH + SSD_XBC
    o_hg = o_dt + SSD_HEADS
    o_s5 = o_hg + HGRN_COLS
    w_ssd = jnp.concatenate(
        [w[:, :o_dt], jnp.pad(w[:, o_dt:o_hg], ((0, 0), (0, SSD_DT_PAD - SSD_HEADS)))], axis=1)
    lb = lb_all[l]
    bd, are, aim, cw = _s5_params(s5_lambda_re[l], s5_lambda_im[l], s5_log_step[l],
                                  s5_b_re[l], s5_b_im[l], s5_c_re[l], s5_c_im[l])
    return dict(
        pre_w=row(pre_norm_w[l]), post_w=row(post_norm_w[l]),
        w_ssd=w_ssd.astype(BF16), w_hg=w[:, o_hg:o_s5].astype(BF16), w_s5=w[:, o_s5:].astype(BF16),
        w_out=w_out[l].astype(BF16),
        conv_w=ssd_conv_w[l].astype(F32), conv_b=row(ssd_conv_b[l]),
        dt_bias=pad_heads(ssd_dt_bias[l]), a=pad_heads(-jnp.exp(ssd_a_log[l].astype(F32))),
        ssd_d=row(jnp.repeat(ssd_d[l].astype(F32), SSD_HEAD_DIM)), ssd_nw=row(ssd_norm_w[l]),
        log_lb=row(jnp.log(lb)), log1m_lb=row(jnp.log1p(-lb)), om_lb=row(1.0 - lb), hg_nw=row(hgrn_norm_w[l]),
        bd=bd, are=are, aim=aim, cw=cw, s5_d=row(s5_d[l]), glu_w=s5_glu_w[l].astype(BF16),
        glu_b=row(s5_glu_b[l]), s5_nw=row(s5_norm_w[l]),
    )


def kernel(x, pre_norm_w, post_norm_w, w_in, w_out, ssd_conv_w, ssd_conv_b, ssd_dt_bias, ssd_a_log, ssd_d, ssd_norm_w, hgrn_lower_bounds, hgrn_norm_w, s5_lambda_re, s5_lambda_im, s5_log_step, s5_b_re, s5_b_im, s5_c_re, s5_c_im, s5_d, s5_glu_w, s5_glu_b, s5_norm_w):
    lb_all = jnp.cumsum(jax.nn.softmax(hgrn_lower_bounds.astype(F32), axis=0), axis=0)
    lb_all = lb_all - lb_all[0:1]
    for l in range(w_in.shape[0]):
        p = _prep_layer(l, lb_all, pre_norm_w, post_norm_w, w_in, w_out, ssd_conv_w, ssd_conv_b,
                        ssd_dt_bias, ssd_a_log, ssd_d, ssd_norm_w, hgrn_norm_w, s5_lambda_re,
                        s5_lambda_im, s5_log_step, s5_b_re, s5_b_im, s5_c_re, s5_c_im, s5_d,
                        s5_glu_w, s5_glu_b, s5_norm_w)
        x = _layer(x, p)
    return x
```

```python
import jax
import jax.numpy as jnp
from jax import lax
from jax.experimental import pallas as pl
from jax.experimental.pallas import tpu as pltpu

F32 = jnp.float32
BF16 = jnp.bfloat16
EPS = 1e-6

D_MODEL = 1024
LANES = 128
SUBLANES = 8
MXU_DIM = 256

SSD_WIDTH = 1024
SSD_HEADS = 16
SSD_HEAD_DIM = 64
SSD_GROUPS = 2
SSD_STATE = 128
SSD_CHUNK = 128
SSD_CONV = 4
SSD_XBC = SSD_WIDTH + 2 * SSD_GROUPS * SSD_STATE
SSD_DT_PAD = LANES
SSD_COLS = SSD_WIDTH + SSD_XBC + SSD_DT_PAD
SSD_PAIRS = SSD_HEADS // 2

HGRN_WIDTH = 512
HGRN_HEADS = 4
HGRN_HEAD_DIM = 128
HGRN_CHUNK = 64
HGRN_COLS = 4 * HGRN_WIDTH
MIX_COLS = SSD_COLS + HGRN_COLS

S5_WIDTH = 512
S5_GROUPS = 32
S5_GROUP_SIZE = 16
S5_STATE = 64
S5_MIN_NEG = 1e-4
S5_COLS = 2 * S5_WIDTH
S5_TILES = S5_GROUPS // 2
S5_NSTATE = S5_TILES * MXU_DIM
S5_STRIP = 4

MIX_TT = 256
S5_TT = 32
VMEM_LIMIT = 56 * 1024 * 1024


def _sigmoid(x):
    return 0.5 * jnp.tanh(0.5 * x) + 0.5


def _silu(x):
    hx = 0.5 * x
    return hx * jnp.tanh(hx) + hx


def _dot(a, b):
    return jnp.dot(a, b, preferred_element_type=F32)


def _dot_nt(a, b):
    return lax.dot_general(a, b, (((1,), (1,)), ((), ())), preferred_element_type=F32)


def _dot_tn(a, b):
    return lax.dot_general(a, b, (((0,), (0,)), ((), ())), preferred_element_type=F32)


def _dot_exact01(m01, x):
    hi = x.astype(BF16)
    r1 = x - hi.astype(F32)
    mid = r1.astype(BF16)
    lo = (r1 - mid.astype(F32)).astype(BF16)
    return _dot(m01, hi) + _dot(m01, mid) + _dot(m01, lo)


def _iota(shape, axis):
    return lax.broadcasted_iota(jnp.int32, shape, axis)


def _cumsum8(x, rows_mod8):
    for sh in (1, 2, 4):
        x = x + jnp.where(rows_mod8 >= sh, pltpu.roll(x, sh, axis=0), 0.0)
    return x


def _rms_normed(x, w):
    ms = jnp.mean(x * x, axis=-1, keepdims=True)
    return x * lax.rsqrt(ms + EPS) * w


def _s5_kernel(x_ref, pw_ref, w_ref, bd_ref, are_ref, aim_ref, cw_ref, dsk_ref, gw_ref, gb_ref, nw_ref,
               o_ref, st_scr, in_scr, bu_scr, x_scr):
    nb, tt, d = x_ref.shape
    rows = tt * nb
    W = S5_WIDTH

    @pl.when(pl.program_id(0) == 0)
    def _():
        st_scr[...] = jnp.zeros(st_scr.shape, F32)

    h = _rms_normed(x_ref[...].reshape(rows, d), pw_ref[...]).astype(BF16)
    proj = _dot(h, w_ref[...])
    for b in range(nb):
        in_scr[:, b, :] = proj[b * tt:(b + 1) * tt, :]

    u = in_scr[:, :, 0:W].reshape(rows, W)
    ub = u.astype(BF16)
    tiles_per_k = MXU_DIM // (2 * S5_GROUP_SIZE)
    for j in range(S5_TILES):
        kt = j // tiles_per_k
        bu = _dot(ub[:, kt * MXU_DIM:(kt + 1) * MXU_DIM], bd_ref[j])
        bu_scr[:, :, j * MXU_DIM:(j + 1) * MXU_DIM] = bu.reshape(tt, nb, MXU_DIM)

    for s0 in range(0, S5_TILES, S5_STRIP):
        c0 = s0 * MXU_DIM
        sw = S5_STRIP * MXU_DIM
        a_re = [jnp.broadcast_to(are_ref[s0 + jj], (nb, LANES)) for jj in range(S5_STRIP)]
        a_im = [jnp.broadcast_to(aim_ref[s0 + jj], (nb, LANES)) for jj in range(S5_STRIP)]

        def step(t, st, c0=c0, sw=sw, a_re=a_re, a_im=a_im):
            b = bu_scr[t, :, c0:c0 + sw]
            parts = []
            for jj in range(S5_STRIP):
                o_re = jj * MXU_DIM
                o_im = o_re + LANES
                xr, xi = st[:, o_re:o_re + LANES], st[:, o_im:o_im + LANES]
                parts.append(a_re[jj] * xr - a_im[jj] * xi + b[:, o_re:o_re + LANES])
                parts.append(a_re[jj] * xi + a_im[jj] * xr + b[:, o_im:o_im + LANES])
            new = jnp.concatenate(parts, axis=1)
            x_scr[pl.ds(pl.multiple_of(t * nb, nb), nb), c0:c0 + sw] = new.astype(BF16)
            return new

        st_scr[:, c0:c0 + sw] = lax.fori_loop(0, tt, step, st_scr[:, c0:c0 + sw], unroll=True)

    ys = []
    tiles_per_n = S5_TILES // (W // MXU_DIM)
    for o in range(W // MXU_DIM):
        acc = jnp.zeros((rows, MXU_DIM), F32)
        for j in range(o * tiles_per_n, (o + 1) * tiles_per_n):
            acc = acc + _dot(x_scr[:, j * MXU_DIM:(j + 1) * MXU_DIM], cw_ref[j])
        ys.append(acc)
    y = jnp.concatenate(ys, axis=1) + dsk_ref[...] * u
    y = 0.5 * y * (1.0 + jnp.tanh(0.7978845608028654 * (y + 0.044715 * (y * y * y))))
    hg = _dot(y.astype(BF16), gw_ref[...]) + gb_ref[...]
    gate = in_scr[:, :, W:2 * W].reshape(rows, W)
    yy = hg[:, 0:W] * _sigmoid(hg[:, W:2 * W]) * _silu(gate)
    out = _rms_normed(yy, nw_ref[...]).reshape(tt, nb, W)
    for t in range(tt):
        o_ref[:, t, :] = out[t]


def _s5(x, pw, w, bd, are, aim, cw, dsk, gw, gb, nw):
    bsz, seqlen, d = x.shape
    tt = min(S5_TT, seqlen)
    const = lambda shape: pl.BlockSpec(shape, lambda i: (0,) * len(shape), pipeline_mode=pl.Buffered(1))
    return pl.pallas_call(
        _s5_kernel,
        grid=(seqlen // tt,),
        in_specs=[
            pl.BlockSpec((bsz, tt, d), lambda i: (0, i, 0)),
            const((1, d)),
            const((d, S5_COLS)),
            const((S5_TILES, MXU_DIM, MXU_DIM)),
            const((S5_TILES, 1, LANES)),
            const((S5_TILES, 1, LANES)),
            const((S5_TILES, MXU_DIM, MXU_DIM)),
            const((1, S5_WIDTH)),
            const((S5_WIDTH, 2 * S5_WIDTH)),
            const((1, 2 * S5_WIDTH)),
            const((1, S5_WIDTH)),
        ],
        out_specs=pl.BlockSpec((bsz, tt, S5_WIDTH), lambda i: (0, i, 0)),
        out_shape=jax.ShapeDtypeStruct((bsz, seqlen, S5_WIDTH), F32),
        scratch_shapes=[
            pltpu.VMEM((bsz, S5_NSTATE), F32),
            pltpu.VMEM((tt, bsz, S5_COLS), F32),
            pltpu.VMEM((tt, bsz, S5_NSTATE), F32),
            pltpu.VMEM((tt * bsz, S5_NSTATE), BF16),
        ],
        compiler_params=pltpu.CompilerParams(
            dimension_semantics=("arbitrary",), vmem_limit_bytes=VMEM_LIMIT),
        name="s5",
    )(x, pw, w, bd, are, aim, cw, dsk, gw, gb, nw)


def _ssd_prologue(p_cur, cw_ref, cb_ref, dtb_ref, xbc_buf, act_scr, dt_scr, zs_scr):
    tt = act_scr.shape[0]
    zs_scr[...] = _silu(p_cur[:, 0:SSD_WIDTH])
    xbc_buf[SUBLANES:SUBLANES + tt, :] = p_cur[:, SSD_WIDTH:SSD_WIDTH + SSD_XBC]
    conv = cb_ref[...]
    for k in range(SSD_CONV):
        off = SUBLANES - (SSD_CONV - 1) + k
        conv = conv + cw_ref[k:k + 1, :] * xbc_buf[off:off + tt, :]
    xbc_buf[0:SUBLANES, :] = xbc_buf[tt:tt + SUBLANES, :]
    act_scr[...] = _silu(conv)
    dtr = p_cur[:, SSD_WIDTH + SSD_XBC:SSD_COLS] + dtb_ref[...]
    dt_scr[...] = jnp.maximum(dtr, 0.0) + jnp.log1p(jnp.exp(-jnp.abs(dtr)))


def _ssd_chunks(a_ref, dsk_ref, act_scr, dt_scr, y_buf, hst):
    tt = act_scr.shape[0]
    T = SSD_CHUNK
    row = _iota((T, T), 0)
    col = _iota((T, T), 1)
    tril = row >= col
    tri01 = jnp.where(tril, 1.0, 0.0).astype(BF16)
    lane_lo = col < SSD_HEAD_DIM
    lane_lo_row = _iota((1, LANES), 1) < SSD_HEAD_DIM
    pairs_per_group = SSD_PAIRS // SSD_GROUPS

    def chunk(c, carry):
        r0 = pl.multiple_of(c * T, T)
        dt_c = dt_scr[pl.ds(r0, T), :]
        a_cum = _dot_exact01(tri01, dt_c * a_ref[...])
        a_last = a_cum[T - 1:T, :]
        ea = jnp.exp(a_cum)
        ea_last = jnp.exp(a_last)
        a_cum_t = a_cum.T
        dt_t = dt_c.T
        w_t = (jnp.exp(a_last - a_cum) * dt_c).T
        for g in range(SSD_GROUPS):
            bg = act_scr[pl.ds(r0, T), SSD_WIDTH + g * SSD_STATE:SSD_WIDTH + (g + 1) * SSD_STATE]
            c0 = SSD_WIDTH + SSD_GROUPS * SSD_STATE + g * SSD_STATE
            cg = act_scr[pl.ds(r0, T), c0:c0 + SSD_STATE]
            scores = _dot_nt(cg.astype(BF16), bg.astype(BF16))
            bg_t = bg.T
            for hp in range(pairs_per_group):
                lt = g * pairs_per_group + hp
                x2 = act_scr[pl.ds(r0, T), lt * LANES:(lt + 1) * LANES]
                h_prev = hst[lt]
                ypair = jnp.zeros((T, LANES), F32)
                spair = jnp.zeros((SSD_STATE, LANES), F32)
                for j in range(2):
                    hh = 2 * lt + j
                    lm = lane_lo if j == 0 else jnp.logical_not(lane_lo)
                    xm = jnp.where(lm, x2, 0.0).astype(BF16)
                    hm = jnp.where(lm, h_prev, 0.0).astype(BF16)
                    seg = a_cum[:, hh:hh + 1] - a_cum_t[hh:hh + 1, :]
                    decay = jnp.where(tril, jnp.exp(jnp.minimum(seg, 0.0)), 0.0)
                    w = scores * decay * dt_t[hh:hh + 1, :]
                    ce = cg * ea[:, hh:hh + 1]
                    lhs = jnp.concatenate([w.astype(BF16), ce.astype(BF16)], axis=1)
                    rhs = jnp.concatenate([xm, hm], axis=0)
                    ypair = ypair + _dot(lhs, rhs)
                    spair = spair + _dot((bg_t * w_t[hh:hh + 1, :]).astype(BF16), xm)
                cd = jnp.where(lane_lo_row, ea_last[:, 2 * lt:2 * lt + 1], ea_last[:, 2 * lt + 1:2 * lt + 2])
                hst[lt] = h_prev * cd + spair
                y_buf[pl.ds(r0, T), lt * LANES:(lt + 1) * LANES] = (
                    ypair + dsk_ref[:, lt * LANES:(lt + 1) * LANES] * x2)
        return carry

    lax.fori_loop(0, tt // T, chunk, 0, unroll=True)


def _ssd_epilogue(nw_ref, y_buf, zs_scr):
    y = y_buf[...] * zs_scr[...]
    gw = SSD_WIDTH // SSD_GROUPS
    parts = []
    for g in range(SSD_GROUPS):
        yg = y[:, g * gw:(g + 1) * gw]
        ms = jnp.mean(yg * yg, axis=-1, keepdims=True)
        parts.append(yg * lax.rsqrt(ms + EPS))
    return (jnp.concatenate(parts, axis=1) * nw_ref[...]).astype(BF16)


def _hgrn_prologue(p_cur, loglb_ref, log1mlb_ref, omlb_ref, q_scr, k_scr, lc_scr, v_scr, gs_scr):
    tt = q_scr.shape[0]
    W = HGRN_WIDTH
    v_scr[...] = p_cur[:, SSD_COLS + 2 * W:SSD_COLS + 3 * W].astype(BF16)
    gs_scr[...] = _silu(p_cur[:, SSD_COLS + 3 * W:SSD_COLS + 4 * W])
    qr = p_cur[:, SSD_COLS:SSD_COLS + W]
    fr = p_cur[:, SSD_COLS + W:SSD_COLS + 2 * W]
    q_scr[...] = _silu(qr)
    log_sig = jnp.minimum(fr, 0.0) - jnp.log1p(jnp.exp(-jnp.abs(fr)))
    t_b = log1mlb_ref[...] + log_sig
    t_a = loglb_ref[...]
    log_f = jnp.maximum(t_a, t_b) + jnp.log1p(jnp.exp(-jnp.abs(t_a - t_b)))
    kk = omlb_ref[...] * _sigmoid(-fr)
    lc = _cumsum8(log_f, _iota((tt, W), 0) % SUBLANES)
    for h in range(HGRN_HEADS):
        lc_scr[h] = lc[:, h * LANES:(h + 1) * LANES]
        k_scr[h] = kk[:, h * LANES:(h + 1) * LANES]


def _hgrn_chunks(v_scr, q_scr, k_scr, lc_scr, s_scr, o_buf):
    tt = q_scr.shape[0]
    C = HGRN_CHUNK
    nsub = C // SUBLANES
    r8 = _iota((SUBLANES, LANES), 0)
    rr = _iota((C, C), 0)
    cc = _iota((C, C), 1)
    rl = _iota((C, LANES), 0)
    rsel = jnp.where(_iota((SUBLANES * LANES, C), 0) // LANES == _iota((SUBLANES * LANES, C), 1) % SUBLANES,
                     1.0, 0.0).astype(BF16)
    diag_mask = ((rr // SUBLANES) == (cc // SUBLANES)) & (rr >= cc)

    def bcast_rows(src, idx, reps):
        return jnp.concatenate(
            [jnp.broadcast_to(src[i:i + 1, :], (reps, LANES)) for i in idx], axis=0)

    def chunk(c, carry):
        r0 = pl.multiple_of(c * C, C)
        for h in range(HGRN_HEADS):
            ls = slice(h * LANES, (h + 1) * LANES)
            q = q_scr[pl.ds(r0, C), ls]
            k = k_scr[h, pl.ds(r0, C), :]
            v = v_scr[pl.ds(r0, C), ls]
            lc_c = lc_scr[h, pl.ds(r0, C), :]
            tot = lc_scr[h, pl.ds(r0 + SUBLANES - 1, nsub, stride=SUBLANES), :]
            ci = _cumsum8(tot, r8)
            ce = ci - tot
            cum = lc_c + bcast_rows(ce, range(nsub), SUBLANES)
            last = ci[nsub - 1:nsub, :]

            s_t = s_scr[h]
            o_c = _dot_nt((q * jnp.exp(cum)).astype(BF16), s_t.astype(BF16))
            kd = (k * jnp.exp(last - cum)).astype(BF16)
            s_scr[h] = s_t * jnp.exp(last) + _dot_tn(v, kd)

            ps = []
            for s in range(SUBLANES):
                ks = jnp.concatenate(
                    [jnp.broadcast_to(k_scr[h, pl.ds(r0 + b * SUBLANES + s, 1), :], (SUBLANES, LANES))
                     for b in range(nsub)], axis=0)
                lcs = jnp.concatenate(
                    [jnp.broadcast_to(lc_scr[h, pl.ds(r0 + b * SUBLANES + s, 1), :], (SUBLANES, LANES))
                     for b in range(nsub)], axis=0)
                ps.append((q * ks * jnp.exp(jnp.minimum(lc_c - lcs, 0.0))).astype(BF16))
            attn = jnp.where(diag_mask, _dot(jnp.concatenate(ps, axis=1), rsel), 0.0)
            for m in (8, 16, 32):
                nblk = C // (2 * m)
                cmid = bcast_rows(ci, [(2 * j + 1) * (m // SUBLANES) - 1 for j in range(nblk)], 2 * m)
                second = (rl % (2 * m)) >= m
                xe = (jnp.where(second, q, k) * jnp.exp(-jnp.abs(cum - cmid))).astype(BF16)
                keep = ((rr % (2 * m)) >= m) & ((cc % (2 * m)) < m) & ((rr // (2 * m)) == (cc // (2 * m)))
                attn = attn + jnp.where(keep, _dot_nt(xe, xe), 0.0)
            o_buf[pl.ds(r0, C), ls] = o_c + _dot(attn.astype(BF16), v)
        return carry

    lax.fori_loop(0, tt // C, chunk, 0, unroll=True)


def _hgrn_epilogue(nw_ref, o_buf, gs_scr):
    o = o_buf[...]
    parts = []
    for h in range(HGRN_HEADS):
        oh = o[:, h * LANES:(h + 1) * LANES]
        ms = jnp.mean(oh * oh, axis=-1, keepdims=True)
        parts.append(oh * lax.rsqrt(ms + EPS))
    return (jnp.concatenate(parts, axis=1) * nw_ref[...] * gs_scr[...]).astype(BF16)


def _mix_kernel(x_ref, xn_ref, ys5_ref, pw_ref, wssd_ref, whg_ref,
                cw_ref, cb_ref, dtb_ref, a_ref, dsk_ref, snw_ref,
                loglb_ref, log1mlb_ref, omlb_ref, hnw_ref, wo_ref, qw_ref, o_ref,
                p_scr, xbc_buf, act_scr, dt_scr, y_buf, hst, q_scr, k_scr, lc_scr, s_scr, o_buf,
                zs_scr, v_scr, gs_scr):
    b = pl.program_id(0)
    i = pl.program_id(1)
    cur = (b * pl.num_programs(1) + i) % 2

    def project(src_ref, dst):
        h = _rms_normed(src_ref[0], pw_ref[...]).astype(BF16)
        dst[:, 0:SSD_COLS] = _dot(h, wssd_ref[...])
        dst[:, SSD_COLS:MIX_COLS] = _dot(h, whg_ref[...])

    @pl.when(i == 0)
    def _():
        xbc_buf[0:SUBLANES, :] = jnp.zeros((SUBLANES, SSD_XBC), F32)
        hst[...] = jnp.zeros(hst.shape, F32)
        s_scr[...] = jnp.zeros(s_scr.shape, F32)

    @pl.when((i == 0) & (b == 0))
    def _():
        project(x_ref, p_scr.at[0])

    p_cur = p_scr.at[cur]
    _ssd_prologue(p_cur, cw_ref, cb_ref, dtb_ref, xbc_buf, act_scr, dt_scr, zs_scr)
    _hgrn_prologue(p_cur, loglb_ref, log1mlb_ref, omlb_ref, q_scr, k_scr, lc_scr, v_scr, gs_scr)
    _ssd_chunks(a_ref, dsk_ref, act_scr, dt_scr, y_buf, hst)
    _hgrn_chunks(v_scr, q_scr, k_scr, lc_scr, s_scr, o_buf)
    y_ssd = _ssd_epilogue(snw_ref, y_buf, zs_scr)
    y_hg = _hgrn_epilogue(hnw_ref, o_buf, gs_scr)
    m = (_dot(y_ssd, wo_ref[0:SSD_WIDTH, :])
         + _dot(y_hg, wo_ref[SSD_WIDTH:SSD_WIDTH + HGRN_WIDTH, :])
         + _dot(ys5_ref[0].astype(BF16), wo_ref[SSD_WIDTH + HGRN_WIDTH:, :]))
    o_ref[0] = x_ref[0] + _rms_normed(m, qw_ref[...])
    project(xn_ref, p_scr.at[1 - cur])


def _mix(x, ys5, p):
    bsz, seqlen, d = x.shape
    tt = min(MIX_TT, seqlen)
    nt = seqlen // tt
    const = lambda shape: pl.BlockSpec(shape, lambda b, i: (0,) * len(shape), pipeline_mode=pl.Buffered(1))
    return pl.pallas_call(
        _mix_kernel,
        grid=(bsz, nt),
        in_specs=[
            pl.BlockSpec((1, tt, d), lambda b, i: (b, i, 0)),
            pl.BlockSpec((1, tt, d), lambda b, i: (jnp.minimum(b + (i + 1) // nt, bsz - 1), (i + 1) % nt, 0)),
            pl.BlockSpec((1, tt, S5_WIDTH), lambda b, i: (b, i, 0)),
            const((1, d)),
            const((d, SSD_COLS)),
            const((d, HGRN_COLS)),
            const((SSD_CONV, SSD_XBC)),
            const((1, SSD_XBC)),
            const((1, SSD_DT_PAD)),
            const((1, SSD_DT_PAD)),
            const((1, SSD_WIDTH)),
            const((1, SSD_WIDTH)),
            const((1, HGRN_WIDTH)), const((1, HGRN_WIDTH)), const((1, HGRN_WIDTH)), const((1, HGRN_WIDTH)),
            const((2 * d, d)),
            const((1, d)),
        ],
        out_specs=pl.BlockSpec((1, tt, d), lambda b, i: (b, i, 0)),
        out_shape=jax.ShapeDtypeStruct((bsz, seqlen, d), F32),
        scratch_shapes=[
            pltpu.VMEM((2, tt, MIX_COLS), F32),
            pltpu.VMEM((tt + SUBLANES, SSD_XBC), F32),
            pltpu.VMEM((tt, SSD_XBC), F32),
            pltpu.VMEM((tt, SSD_DT_PAD), F32),
            pltpu.VMEM((tt, SSD_WIDTH), F32),
            pltpu.VMEM((SSD_PAIRS, SSD_STATE, LANES), F32),
            pltpu.VMEM((tt, HGRN_WIDTH), F32),
            pltpu.VMEM((HGRN_HEADS, tt, LANES), F32),
            pltpu.VMEM((HGRN_HEADS, tt, LANES), F32),
            pltpu.VMEM((HGRN_HEADS, HGRN_HEAD_DIM, HGRN_HEAD_DIM), F32),
            pltpu.VMEM((tt, HGRN_WIDTH), F32),
            pltpu.VMEM((tt, SSD_WIDTH), F32),
            pltpu.VMEM((tt, HGRN_WIDTH), BF16),
            pltpu.VMEM((tt, HGRN_WIDTH), F32),
        ],
        compiler_params=pltpu.CompilerParams(
            dimension_semantics=("arbitrary", "arbitrary"), vmem_limit_bytes=VMEM_LIMIT),
        name="mix",
    )(x, x, ys5, p["pre_w"], p["w_ssd"], p["w_hg"],
      p["conv_w"], p["conv_b"], p["dt_bias"], p["a"], p["ssd_d"], p["ssd_nw"],
      p["log_lb"], p["log1m_lb"], p["om_lb"], p["hg_nw"], p["w_out"], p["post_w"])


def _s5_params(lam_re, lam_im, log_step, b_re, b_im, c_re, c_im):
    lr = jnp.minimum(lam_re.astype(F32), -S5_MIN_NEG)
    li = lam_im.astype(F32)
    step = jnp.exp(log_step.astype(F32))[:, None]
    mag = jnp.exp(lr * step)
    ang = li * step
    ab_re, ab_im = mag * jnp.cos(ang), mag * jnp.sin(ang)
    den = lr * lr + li * li
    num_re, num_im = ab_re - 1.0, ab_im
    co_re = (num_re * lr + num_im * li) / den
    co_im = (num_im * lr - num_re * li) / den
    bb_re = co_re[..., None] * b_re - co_im[..., None] * b_im
    bb_im = co_re[..., None] * b_im + co_im[..., None] * b_re
    P, H = S5_STATE, S5_GROUP_SIZE
    pair = lambda t: t.reshape(S5_TILES, 2, *t.shape[1:])
    eye2 = jnp.eye(2, dtype=F32)
    blk_re = jnp.einsum("jgph,gk->jghkp", pair(bb_re), eye2).reshape(S5_TILES, 2 * H, 2 * P)
    blk_im = jnp.einsum("jgph,gk->jghkp", pair(bb_im), eye2).reshape(S5_TILES, 2 * H, 2 * P)
    blk = jnp.concatenate([blk_re, blk_im], axis=-1)
    tiles_per_k = MXU_DIM // (2 * H)
    slot = jax.nn.one_hot(jnp.arange(S5_TILES) % tiles_per_k, tiles_per_k, dtype=F32)
    bd = jnp.einsum("js,jrc->jsrc", slot, blk).reshape(S5_TILES, MXU_DIM, MXU_DIM)
    out_re = jnp.einsum("jghp,gk->jgpkh", pair(c_re.astype(F32)), eye2).reshape(S5_TILES, 2 * P, 2 * H)
    out_im = jnp.einsum("jghp,gk->jgpkh", pair(c_im.astype(F32)), eye2).reshape(S5_TILES, 2 * P, 2 * H)
    oblk = jnp.concatenate([out_re, -out_im], axis=1)
    cw = jnp.einsum("js,jrc->jrsc", slot, oblk).reshape(S5_TILES, MXU_DIM, MXU_DIM)
    are = pair(ab_re).reshape(S5_TILES, 1, 2 * P)
    aim = pair(ab_im).reshape(S5_TILES, 1, 2 * P)
    return bd.astype(BF16), are, aim, cw.astype(BF16)


def _layer(x, p):
    y_s5 = _s5(x, p["pre_w"], p["w_s5"], p["bd"], p["are"], p["aim"], p["cw"], p["s5_d"],
               p["glu_w"], p["glu_b"], p["s5_nw"])
    return _mix(x, y_s5, p)


def _prep_layer(l, lb_all, pre_norm_w, post_norm_w, w_in, w_out, ssd_conv_w, ssd_conv_b, ssd_dt_bias,
                ssd_a_log, ssd_d, ssd_norm_w, hgrn_norm_w, s5_lambda_re, s5_lambda_im, s5_log_step,
                s5_b_re, s5_b_im, s5_c_re, s5_c_im, s5_d, s5_glu_w, s5_glu_b, s5_norm_w):
    row = lambda v: v.astype(F32).reshape(1, -1)
    pad_heads = lambda v: jnp.pad(v.astype(F32), (0, SSD_DT_PAD - SSD_HEADS)).reshape(1, -1)
    w = w_in[l]
    o_dt = SSD_WIDTH + SSD_XBC
    o_hg = o_dt + SSD_HEADS
    o_s5 = o_hg + HGRN_COLS
    w_ssd = jnp.concatenate(
        [w[:, :o_dt], jnp.pad(w[:, o_dt:o_hg], ((0, 0), (0, SSD_DT_PAD - SSD_HEADS)))], axis=1)
    lb = lb_all[l]
    bd, are, aim, cw = _s5_params(s5_lambda_re[l], s5_lambda_im[l], s5_log_step[l],
                                  s5_b_re[l], s5_b_im[l], s5_c_re[l], s5_c_im[l])
    return dict(
        pre_w=row(pre_norm_w[l]), post_w=row(post_norm_w[l]),
        w_ssd=w_ssd.astype(BF16), w_hg=w[:, o_hg:o_s5].astype(BF16), w_s5=w[:, o_s5:].astype(BF16),
        w_out=w_out[l].astype(BF16),
        conv_w=ssd_conv_w[l].astype(F32), conv_b=row(ssd_conv_b[l]),
        dt_bias=pad_heads(ssd_dt_bias[l]), a=pad_heads(-jnp.exp(ssd_a_log[l].astype(F32))),
        ssd_d=row(jnp.repeat(ssd_d[l].astype(F32), SSD_HEAD_DIM)), ssd_nw=row(ssd_norm_w[l]),
        log_lb=row(jnp.log(lb)), log1m_lb=row(jnp.log1p(-lb)), om_lb=row(1.0 - lb), hg_nw=row(hgrn_norm_w[l]),
        bd=bd, are=are, aim=aim, cw=cw, s5_d=row(s5_d[l]), glu_w=s5_glu_w[l].astype(BF16),
        glu_b=row(s5_glu_b[l]), s5_nw=row(s5_norm_w[l]),
    )


def kernel(x, pre_norm_w, post_norm_w, w_in, w_out, ssd_conv_w, ssd_conv_b, ssd_dt_bias, ssd_a_log, ssd_d, ssd_norm_w, hgrn_lower_bounds, hgrn_norm_w, s5_lambda_re, s5_lambda_im, s5_log_step, s5_b_re, s5_b_im, s5_c_re, s5_c_im, s5_d, s5_glu_w, s5_glu_b, s5_norm_w):
    lb_all = jnp.cumsum(jax.nn.softmax(hgrn_lower_bounds.astype(F32), axis=0), axis=0)
    lb_all = lb_all - lb_all[0:1]
    for l in range(w_in.shape[0]):
        p = _prep_layer(l, lb_all, pre_norm_w, post_norm_w, w_in, w_out, ssd_conv_w, ssd_conv_b,
                        ssd_dt_bias, ssd_a_log, ssd_d, ssd_norm_w, hgrn_norm_w, s5_lambda_re,
                        s5_lambda_im, s5_log_step, s5_b_re, s5_b_im, s5_c_re, s5_c_im, s5_d,
                        s5_glu_w, s5_glu_b, s5_norm_w)
        x = _layer(x, p)
    return x
```

```python
import jax
import jax.numpy as jnp
from jax import lax
from jax.experimental import pallas as pl
from jax.experimental.pallas import tpu as pltpu

F32 = jnp.float32
BF16 = jnp.bfloat16
EPS = 1e-6
LOG2E = 1.4426950408889634

D_MODEL = 1024
LANES = 128
SUBLANES = 8
MXU_DIM = 256

SSD_WIDTH = 1024
SSD_HEADS = 16
SSD_HEAD_DIM = 64
SSD_GROUPS = 2
SSD_STATE = 128
SSD_CHUNK = 128
SSD_CONV = 4
SSD_XBC = SSD_WIDTH + 2 * SSD_GROUPS * SSD_STATE
SSD_DT_PAD = LANES
SSD_COLS = SSD_WIDTH + SSD_XBC + SSD_DT_PAD
SSD_PAIRS = SSD_HEADS // 2

HGRN_WIDTH = 512
HGRN_HEADS = 4
HGRN_HEAD_DIM = 128
HGRN_CHUNK = 64
HGRN_COLS = 4 * HGRN_WIDTH
MIX_COLS = SSD_COLS + HGRN_COLS

S5_WIDTH = 512
S5_GROUPS = 32
S5_GROUP_SIZE = 16
S5_STATE = 64
S5_MIN_NEG = 1e-4
S5_COLS = 2 * S5_WIDTH
S5_TILES = S5_GROUPS // 2
S5_NSTATE = S5_TILES * MXU_DIM
S5_STRIP = 4

MIX_TT = 256
S5_TT = 32
VMEM_LIMIT = 56 * 1024 * 1024


def _sigmoid(x):
    return 0.5 * jnp.tanh(0.5 * x) + 0.5


def _silu(x):
    hx = 0.5 * x
    return hx * jnp.tanh(hx) + hx


def _dot(a, b):
    return jnp.dot(a, b, preferred_element_type=F32)


def _dot_nt(a, b):
    return lax.dot_general(a, b, (((1,), (1,)), ((), ())), preferred_element_type=F32)


def _dot_tn(a, b):
    return lax.dot_general(a, b, (((0,), (0,)), ((), ())), preferred_element_type=F32)


def _dot_exact01(m01, x):
    hi = x.astype(BF16)
    r1 = x - hi.astype(F32)
    mid = r1.astype(BF16)
    lo = (r1 - mid.astype(F32)).astype(BF16)
    return _dot(m01, hi) + _dot(m01, mid) + _dot(m01, lo)


def _iota(shape, axis):
    return lax.broadcasted_iota(jnp.int32, shape, axis)


def _cumsum8(x, rows_mod8):
    for sh in (1, 2, 4):
        x = x + jnp.where(rows_mod8 >= sh, pltpu.roll(x, sh, axis=0), 0.0)
    return x


def _rms_normed(x, w):
    ms = jnp.mean(x * x, axis=-1, keepdims=True)
    return x * lax.rsqrt(ms + EPS) * w


def _s5_kernel(x_ref, pw_ref, w_ref, bd_ref, are_ref, aim_ref, cw_ref, dsk_ref, gw_ref, gb_ref, nw_ref,
               o_ref, st_scr, in_scr, bu_scr, x_scr):
    nb, tt, d = x_ref.shape
    rows = tt * nb
    W = S5_WIDTH

    @pl.when(pl.program_id(0) == 0)
    def _():
        st_scr[...] = jnp.zeros(st_scr.shape, F32)

    h = _rms_normed(x_ref[...].reshape(rows, d), pw_ref[...]).astype(BF16)
    proj = _dot(h, w_ref[...])
    for b in range(nb):
        in_scr[:, b, :] = proj[b * tt:(b + 1) * tt, :]

    u = in_scr[:, :, 0:W].reshape(rows, W)
    ub = u.astype(BF16)
    tiles_per_k = MXU_DIM // (2 * S5_GROUP_SIZE)
    for j in range(S5_TILES):
        kt = j // tiles_per_k
        bu = _dot(ub[:, kt * MXU_DIM:(kt + 1) * MXU_DIM], bd_ref[j])
        bu_scr[:, :, j * MXU_DIM:(j + 1) * MXU_DIM] = bu.reshape(tt, nb, MXU_DIM)

    for s0 in range(0, S5_TILES, S5_STRIP):
        c0 = s0 * MXU_DIM
        sw = S5_STRIP * MXU_DIM
        a_re = [jnp.broadcast_to(are_ref[s0 + jj], (nb, LANES)) for jj in range(S5_STRIP)]
        a_im = [jnp.broadcast_to(aim_ref[s0 + jj], (nb, LANES)) for jj in range(S5_STRIP)]

        def step(t, st, c0=c0, sw=sw, a_re=a_re, a_im=a_im):
            b = bu_scr[t, :, c0:c0 + sw]
            parts = []
            for jj in range(S5_STRIP):
                o_re = jj * MXU_DIM
                o_im = o_re + LANES
                xr, xi = st[:, o_re:o_re + LANES], st[:, o_im:o_im + LANES]
                parts.append(a_re[jj] * xr - a_im[jj] * xi + b[:, o_re:o_re + LANES])
                parts.append(a_re[jj] * xi + a_im[jj] * xr + b[:, o_im:o_im + LANES])
            new = jnp.concatenate(parts, axis=1)
            x_scr[pl.ds(pl.multiple_of(t * nb, nb), nb), c0:c0 + sw] = new.astype(BF16)
            return new

        st_scr[:, c0:c0 + sw] = lax.fori_loop(0, tt, step, st_scr[:, c0:c0 + sw], unroll=True)

    ys = []
    tiles_per_n = S5_TILES // (W // MXU_DIM)
    for o in range(W // MXU_DIM):
        acc = jnp.zeros((rows, MXU_DIM), F32)
        for j in range(o * tiles_per_n, (o + 1) * tiles_per_n):
            acc = acc + _dot(x_scr[:, j * MXU_DIM:(j + 1) * MXU_DIM], cw_ref[j])
        ys.append(acc)
    y = jnp.concatenate(ys, axis=1) + dsk_ref[...] * u
    y = 0.5 * y * (1.0 + jnp.tanh(0.7978845608028654 * (y + 0.044715 * (y * y * y))))
    hg = _dot(y.astype(BF16), gw_ref[...]) + gb_ref[...]
    gate = in_scr[:, :, W:2 * W].reshape(rows, W)
    yy = hg[:, 0:W] * _sigmoid(hg[:, W:2 * W]) * _silu(gate)
    out = _rms_normed(yy, nw_ref[...]).reshape(tt, nb, W)
    for t in range(tt):
        o_ref[:, t, :] = out[t]


def _s5(x, pw, w, bd, are, aim, cw, dsk, gw, gb, nw):
    bsz, seqlen, d = x.shape
    tt = min(S5_TT, seqlen)
    const = lambda shape: pl.BlockSpec(shape, lambda i: (0,) * len(shape), pipeline_mode=pl.Buffered(1))
    return pl.pallas_call(
        _s5_kernel,
        grid=(seqlen // tt,),
        in_specs=[
            pl.BlockSpec((bsz, tt, d), lambda i: (0, i, 0)),
            const((1, d)),
            const((d, S5_COLS)),
            const((S5_TILES, MXU_DIM, MXU_DIM)),
            const((S5_TILES, 1, LANES)),
            const((S5_TILES, 1, LANES)),
            const((S5_TILES, MXU_DIM, MXU_DIM)),
            const((1, S5_WIDTH)),
            const((S5_WIDTH, 2 * S5_WIDTH)),
            const((1, 2 * S5_WIDTH)),
            const((1, S5_WIDTH)),
        ],
        out_specs=pl.BlockSpec((bsz, tt, S5_WIDTH), lambda i: (0, i, 0)),
        out_shape=jax.ShapeDtypeStruct((bsz, seqlen, S5_WIDTH), F32),
        scratch_shapes=[
            pltpu.VMEM((bsz, S5_NSTATE), F32),
            pltpu.VMEM((tt, bsz, S5_COLS), F32),
            pltpu.VMEM((tt, bsz, S5_NSTATE), F32),
            pltpu.VMEM((tt * bsz, S5_NSTATE), BF16),
        ],
        compiler_params=pltpu.CompilerParams(
            dimension_semantics=("arbitrary",), vmem_limit_bytes=VMEM_LIMIT),
        name="s5",
    )(x, pw, w, bd, are, aim, cw, dsk, gw, gb, nw)


def _ssd_prologue(p_cur, cw_ref, cb_ref, dtb_ref, conv_carry, act_scr, dt_scr, zs_scr):
    tt = act_scr.shape[0]
    zs_scr[...] = _silu(p_cur[:, 0:SSD_WIDTH])
    x = p_cur[:, SSD_WIDTH:SSD_WIDTH + SSD_XBC]
    first_row = _iota((SUBLANES, SSD_XBC), 0) == 0
    acc = cw_ref[0:1, :] * x
    for k in range(1, SSD_CONV):
        prev_last = conv_carry[k - 1:k, :]
        conv_carry[k - 1:k, :] = acc[tt - 1:tt, :]
        rolled = pltpu.roll(acc, 1, axis=0)
        head = jnp.where(first_row, prev_last, rolled[0:SUBLANES, :])
        acc = jnp.concatenate([head, rolled[SUBLANES:, :]], axis=0) + cw_ref[k:k + 1, :] * x
    act_scr[...] = _silu(acc + cb_ref[...])
    dtr = p_cur[:, SSD_WIDTH + SSD_XBC:SSD_COLS] + dtb_ref[...]
    dt_scr[...] = jnp.maximum(dtr, 0.0) + jnp.log(1.0 + jnp.exp(-jnp.abs(dtr)))


def _ssd_chunks(a_ref, dsk_ref, act_scr, dt_scr, y_buf, hst):
    tt = act_scr.shape[0]
    T = SSD_CHUNK
    row = _iota((T, T), 0)
    col = _iota((T, T), 1)
    tril = row >= col
    tri01 = jnp.where(tril, 1.0, 0.0).astype(BF16)
    lane_lo = col < SSD_HEAD_DIM
    lane_lo_row = _iota((1, LANES), 1) < SSD_HEAD_DIM
    pairs_per_group = SSD_PAIRS // SSD_GROUPS

    def chunk(c, carry):
        r0 = pl.multiple_of(c * T, T)
        dt_c = dt_scr[pl.ds(r0, T), :]
        a_cum = _dot_exact01(tri01, dt_c * a_ref[...])
        a_last = a_cum[T - 1:T, :]
        ea = jnp.exp2(a_cum)
        ea_last = jnp.exp2(a_last)
        a_cum_t = a_cum.T
        dt_t = dt_c.T
        w_t = (jnp.exp2(a_last - a_cum) * dt_c).T
        for g in range(SSD_GROUPS):
            bg = act_scr[pl.ds(r0, T), SSD_WIDTH + g * SSD_STATE:SSD_WIDTH + (g + 1) * SSD_STATE]
            c0 = SSD_WIDTH + SSD_GROUPS * SSD_STATE + g * SSD_STATE
            cg = act_scr[pl.ds(r0, T), c0:c0 + SSD_STATE]
            scores = _dot_nt(cg.astype(BF16), bg.astype(BF16))
            bg_t = bg.T
            for hp in range(pairs_per_group):
                lt = g * pairs_per_group + hp
                x2 = act_scr[pl.ds(r0, T), lt * LANES:(lt + 1) * LANES]
                h_prev = hst[lt]
                ypair = jnp.zeros((T, LANES), F32)
                spair = jnp.zeros((SSD_STATE, LANES), F32)
                for j in range(2):
                    hh = 2 * lt + j
                    lm = lane_lo if j == 0 else jnp.logical_not(lane_lo)
                    xm = jnp.where(lm, x2, 0.0).astype(BF16)
                    hm = jnp.where(lm, h_prev, 0.0).astype(BF16)
                    seg = a_cum[:, hh:hh + 1] - a_cum_t[hh:hh + 1, :]
                    decay = jnp.where(tril, jnp.exp2(jnp.minimum(seg, 0.0)), 0.0)
                    w = scores * decay * dt_t[hh:hh + 1, :]
                    ce = cg * ea[:, hh:hh + 1]
                    lhs = jnp.concatenate([w.astype(BF16), ce.astype(BF16)], axis=1)
                    rhs = jnp.concatenate([xm, hm], axis=0)
                    ypair = ypair + _dot(lhs, rhs)
                    spair = spair + _dot((bg_t * w_t[hh:hh + 1, :]).astype(BF16), xm)
                cd = jnp.where(lane_lo_row, ea_last[:, 2 * lt:2 * lt + 1], ea_last[:, 2 * lt + 1:2 * lt + 2])
                hst[lt] = h_prev * cd + spair
                y_buf[pl.ds(r0, T), lt * LANES:(lt + 1) * LANES] = (
                    ypair + dsk_ref[:, lt * LANES:(lt + 1) * LANES] * x2)
        return carry

    lax.fori_loop(0, tt // T, chunk, 0, unroll=True)


def _ssd_epilogue(nw_ref, y_buf, zs_scr):
    y = y_buf[...] * zs_scr[...]
    gw = SSD_WIDTH // SSD_GROUPS
    parts = []
    for g in range(SSD_GROUPS):
        yg = y[:, g * gw:(g + 1) * gw]
        ms = jnp.mean(yg * yg, axis=-1, keepdims=True)
        parts.append(yg * lax.rsqrt(ms + EPS))
    return (jnp.concatenate(parts, axis=1) * nw_ref[...]).astype(BF16)


def _hgrn_prologue(p_cur, loglb_ref, log1mlb_ref, omlb_ref, q_scr, k_scr, lc_scr, v_scr, gs_scr):
    tt = q_scr.shape[0]
    W = HGRN_WIDTH
    v_scr[...] = p_cur[:, SSD_COLS + 2 * W:SSD_COLS + 3 * W].astype(BF16)
    gs_scr[...] = _silu(p_cur[:, SSD_COLS + 3 * W:SSD_COLS + 4 * W])
    qr = p_cur[:, SSD_COLS:SSD_COLS + W]
    fr = p_cur[:, SSD_COLS + W:SSD_COLS + 2 * W]
    q_scr[...] = _silu(qr)
    log_sig = jnp.minimum(fr, 0.0) - jnp.log(1.0 + jnp.exp(-jnp.abs(fr)))
    t_b = log1mlb_ref[...] + log_sig
    t_a = loglb_ref[...]
    log_f = jnp.maximum(t_a, t_b) + jnp.log(1.0 + jnp.exp(-jnp.abs(t_a - t_b)))
    kk = omlb_ref[...] * _sigmoid(-fr)
    lc = _cumsum8(LOG2E * log_f, _iota((tt, W), 0) % SUBLANES)
    for h in range(HGRN_HEADS):
        lc_scr[h] = lc[:, h * LANES:(h + 1) * LANES]
        k_scr[h] = kk[:, h * LANES:(h + 1) * LANES]


def _hgrn_chunks(v_scr, q_scr, k_scr, lc_scr, s_scr, o_buf):
    tt = q_scr.shape[0]
    C = HGRN_CHUNK
    nsub = C // SUBLANES
    r8 = _iota((SUBLANES, LANES), 0)
    rr = _iota((C, C), 0)
    cc = _iota((C, C), 1)
    rl = _iota((C, LANES), 0)
    rsel = jnp.where(_iota((SUBLANES * LANES, C), 0) // LANES == _iota((SUBLANES * LANES, C), 1) % SUBLANES,
                     1.0, 0.0).astype(BF16)
    diag_mask = ((rr // SUBLANES) == (cc // SUBLANES)) & (rr >= cc)

    def bcast_rows(src, idx, reps):
        return jnp.concatenate(
            [jnp.broadcast_to(src[i:i + 1, :], (reps, LANES)) for i in idx], axis=0)

    def chunk(c, carry):
        r0 = pl.multiple_of(c * C, C)
        for h in range(HGRN_HEADS):
            ls = slice(h * LANES, (h + 1) * LANES)
            q = q_scr[pl.ds(r0, C), ls]
            k = k_scr[h, pl.ds(r0, C), :]
            v = v_scr[pl.ds(r0, C), ls]
            lc_c = lc_scr[h, pl.ds(r0, C), :]
            tot = lc_scr[h, pl.ds(r0 + SUBLANES - 1, nsub, stride=SUBLANES), :]
            ci = _cumsum8(tot, r8)
            ce = ci - tot
            cum = lc_c + bcast_rows(ce, range(nsub), SUBLANES)
            last = ci[nsub - 1:nsub, :]

            s_t = s_scr[h]
            o_c = _dot_nt((q * jnp.exp2(cum)).astype(BF16), s_t.astype(BF16))
            kd = (k * jnp.exp2(last - cum)).astype(BF16)
            s_scr[h] = s_t * jnp.exp2(last) + _dot_tn(v, kd)

            ps = []
            for s in range(SUBLANES):
                ks = jnp.concatenate(
                    [jnp.broadcast_to(k_scr[h, pl.ds(r0 + b * SUBLANES + s, 1), :], (SUBLANES, LANES))
                     for b in range(nsub)], axis=0)
                lcs = jnp.concatenate(
                    [jnp.broadcast_to(lc_scr[h, pl.ds(r0 + b * SUBLANES + s, 1), :], (SUBLANES, LANES))
                     for b in range(nsub)], axis=0)
                ps.append((q * ks * jnp.exp2(jnp.minimum(lc_c - lcs, 0.0))).astype(BF16))
            attn = jnp.where(diag_mask, _dot(jnp.concatenate(ps, axis=1), rsel), 0.0)
            for m in (8, 16, 32):
                nblk = C // (2 * m)
                cmid = bcast_rows(ci, [(2 * j + 1) * (m // SUBLANES) - 1 for j in range(nblk)], 2 * m)
                second = (rl % (2 * m)) >= m
                xe = (jnp.where(second, q, k) * jnp.exp2(-jnp.abs(cum - cmid))).astype(BF16)
                keep = ((rr % (2 * m)) >= m) & ((cc % (2 * m)) < m) & ((rr // (2 * m)) == (cc // (2 * m)))
                attn = attn + jnp.where(keep, _dot_nt(xe, xe), 0.0)
            o_buf[pl.ds(r0, C), ls] = o_c + _dot(attn.astype(BF16), v)
        return carry

    lax.fori_loop(0, tt // C, chunk, 0, unroll=True)


def _hgrn_epilogue(nw_ref, o_buf, gs_scr):
    o = o_buf[...]
    parts = []
    for h in range(HGRN_HEADS):
        oh = o[:, h * LANES:(h + 1) * LANES]
        ms = jnp.mean(oh * oh, axis=-1, keepdims=True)
        parts.append(oh * lax.rsqrt(ms + EPS))
    return (jnp.concatenate(parts, axis=1) * nw_ref[...] * gs_scr[...]).astype(BF16)


def _mix_kernel(x_ref, xn_ref, ys5_ref, pw_ref, wssd_ref, whg_ref,
                cw_ref, cb_ref, dtb_ref, a_ref, dsk_ref, snw_ref,
                loglb_ref, log1mlb_ref, omlb_ref, hnw_ref, wo_ref, qw_ref, o_ref,
                p_scr, xbc_buf, act_scr, dt_scr, y_buf, hst, q_scr, k_scr, lc_scr, s_scr, o_buf,
                zs_scr, v_scr, gs_scr):
    b = pl.program_id(0)
    i = pl.program_id(1)
    cur = (b * pl.num_programs(1) + i) % 2

    def project(src_ref, dst):
        h = _rms_normed(src_ref[0], pw_ref[...]).astype(BF16)
        dst[:, 0:SSD_COLS] = _dot(h, wssd_ref[...])
        dst[:, SSD_COLS:MIX_COLS] = _dot(h, whg_ref[...])

    @pl.when(i == 0)
    def _():
        xbc_buf[...] = jnp.zeros((SUBLANES, SSD_XBC), F32)
        hst[...] = jnp.zeros(hst.shape, F32)
        s_scr[...] = jnp.zeros(s_scr.shape, F32)

    @pl.when((i == 0) & (b == 0))
    def _():
        project(x_ref, p_scr.at[0])

    p_cur = p_scr.at[cur]
    _ssd_prologue(p_cur, cw_ref, cb_ref, dtb_ref, xbc_buf, act_scr, dt_scr, zs_scr)
    _hgrn_prologue(p_cur, loglb_ref, log1mlb_ref, omlb_ref, q_scr, k_scr, lc_scr, v_scr, gs_scr)
    _ssd_chunks(a_ref, dsk_ref, act_scr, dt_scr, y_buf, hst)
    _hgrn_chunks(v_scr, q_scr, k_scr, lc_scr, s_scr, o_buf)
    y_ssd = _ssd_epilogue(snw_ref, y_buf, zs_scr)
    y_hg = _hgrn_epilogue(hnw_ref, o_buf, gs_scr)
    m = (_dot(y_ssd, wo_ref[0:SSD_WIDTH, :])
         + _dot(y_hg, wo_ref[SSD_WIDTH:SSD_WIDTH + HGRN_WIDTH, :])
         + _dot(ys5_ref[0].astype(BF16), wo_ref[SSD_WIDTH + HGRN_WIDTH:, :]))
    o_ref[0] = x_ref[0] + _rms_normed(m, qw_ref[...])
    project(xn_ref, p_scr.at[1 - cur])


def _mix(x, ys5, p):
    bsz, seqlen, d = x.shape
    tt = min(MIX_TT, seqlen)
    nt = seqlen // tt
    const = lambda shape: pl.BlockSpec(shape, lambda b, i: (0,) * len(shape), pipeline_mode=pl.Buffered(1))
    return pl.pallas_call(
        _mix_kernel,
        grid=(bsz, nt),
        in_specs=[
            pl.BlockSpec((1, tt, d), lambda b, i: (b, i, 0)),
            pl.BlockSpec((1, tt, d), lambda b, i: (jnp.minimum(b + (i + 1) // nt, bsz - 1), (i + 1) % nt, 0)),
            pl.BlockSpec((1, tt, S5_WIDTH), lambda b, i: (b, i, 0)),
            const((1, d)),
            const((d, SSD_COLS)),
            const((d, HGRN_COLS)),
            const((SSD_CONV, SSD_XBC)),
            const((1, SSD_XBC)),
            const((1, SSD_DT_PAD)),
            const((1, SSD_DT_PAD)),
            const((1, SSD_WIDTH)),
            const((1, SSD_WIDTH)),
            const((1, HGRN_WIDTH)), const((1, HGRN_WIDTH)), const((1, HGRN_WIDTH)), const((1, HGRN_WIDTH)),
            const((2 * d, d)),
            const((1, d)),
        ],
        out_specs=pl.BlockSpec((1, tt, d), lambda b, i: (b, i, 0)),
        out_shape=jax.ShapeDtypeStruct((bsz, seqlen, d), F32),
        scratch_shapes=[
            pltpu.VMEM((2, tt, MIX_COLS), F32),
            pltpu.VMEM((SUBLANES, SSD_XBC), F32),
            pltpu.VMEM((tt, SSD_XBC), F32),
            pltpu.VMEM((tt, SSD_DT_PAD), F32),
            pltpu.VMEM((tt, SSD_WIDTH), F32),
            pltpu.VMEM((SSD_PAIRS, SSD_STATE, LANES), F32),
            pltpu.VMEM((tt, HGRN_WIDTH), F32),
            pltpu.VMEM((HGRN_HEADS, tt, LANES), F32),
            pltpu.VMEM((HGRN_HEADS, tt, LANES), F32),
            pltpu.VMEM((HGRN_HEADS, HGRN_HEAD_DIM, HGRN_HEAD_DIM), F32),
            pltpu.VMEM((tt, HGRN_WIDTH), F32),
            pltpu.VMEM((tt, SSD_WIDTH), F32),
            pltpu.VMEM((tt, HGRN_WIDTH), BF16),
            pltpu.VMEM((tt, HGRN_WIDTH), F32),
        ],
        compiler_params=pltpu.CompilerParams(
            dimension_semantics=("arbitrary", "arbitrary"), vmem_limit_bytes=VMEM_LIMIT),
        name="mix",
    )(x, x, ys5, p["pre_w"], p["w_ssd"], p["w_hg"],
      p["conv_w"], p["conv_b"], p["dt_bias"], p["a"], p["ssd_d"], p["ssd_nw"],
      p["log_lb"], p["log1m_lb"], p["om_lb"], p["hg_nw"], p["w_out"], p["post_w"])


def _s5_params(lam_re, lam_im, log_step, b_re, b_im, c_re, c_im):
    lr = jnp.minimum(lam_re.astype(F32), -S5_MIN_NEG)
    li = lam_im.astype(F32)
    step = jnp.exp(log_step.astype(F32))[:, None]
    mag = jnp.exp(lr * step)
    ang = li * step
    ab_re, ab_im = mag * jnp.cos(ang), mag * jnp.sin(ang)
    den = lr * lr + li * li
    num_re, num_im = ab_re - 1.0, ab_im
    co_re = (num_re * lr + num_im * li) / den
    co_im = (num_im * lr - num_re * li) / den
    bb_re = co_re[..., None] * b_re - co_im[..., None] * b_im
    bb_im = co_re[..., None] * b_im + co_im[..., None] * b_re
    P, H = S5_STATE, S5_GROUP_SIZE
    pair = lambda t: t.reshape(S5_TILES, 2, *t.shape[1:])
    eye2 = jnp.eye(2, dtype=F32)
    blk_re = jnp.einsum("jgph,gk->jghkp", pair(bb_re), eye2).reshape(S5_TILES, 2 * H, 2 * P)
    blk_im = jnp.einsum("jgph,gk->jghkp", pair(bb_im), eye2).reshape(S5_TILES, 2 * H, 2 * P)
    blk = jnp.concatenate([blk_re, blk_im], axis=-1)
    tiles_per_k = MXU_DIM // (2 * H)
    slot = jax.nn.one_hot(jnp.arange(S5_TILES) % tiles_per_k, tiles_per_k, dtype=F32)
    bd = jnp.einsum("js,jrc->jsrc", slot, blk).reshape(S5_TILES, MXU_DIM, MXU_DIM)
    out_re = jnp.einsum("jghp,gk->jgpkh", pair(c_re.astype(F32)), eye2).reshape(S5_TILES, 2 * P, 2 * H)
    out_im = jnp.einsum("jghp,gk->jgpkh", pair(c_im.astype(F32)), eye2).reshape(S5_TILES, 2 * P, 2 * H)
    oblk = jnp.concatenate([out_re, -out_im], axis=1)
    cw = jnp.einsum("js,jrc->jrsc", slot, oblk).reshape(S5_TILES, MXU_DIM, MXU_DIM)
    are = pair(ab_re).reshape(S5_TILES, 1, 2 * P)
    aim = pair(ab_im).reshape(S5_TILES, 1, 2 * P)
    return bd.astype(BF16), are, aim, cw.astype(BF16)


def _layer(x, p):
    y_s5 = _s5(x, p["pre_w"], p["w_s5"], p["bd"], p["are"], p["aim"], p["cw"], p["s5_d"],
               p["glu_w"], p["glu_b"], p["s5_nw"])
    return _mix(x, y_s5, p)


def _prep_layer(l, lb_all, pre_norm_w, post_norm_w, w_in, w_out, ssd_conv_w, ssd_conv_b, ssd_dt_bias,
                ssd_a_log, ssd_d, ssd_norm_w, hgrn_norm_w, s5_lambda_re, s5_lambda_im, s5_log_step,
                s5_b_re, s5_b_im, s5_c_re, s5_c_im, s5_d, s5_glu_w, s5_glu_b, s5_norm_w):
    row = lambda v: v.astype(F32).reshape(1, -1)
    pad_heads = lambda v: jnp.pad(v.astype(F32), (0, SSD_DT_PAD - SSD_HEADS)).reshape(1, -1)
    w = w_in[l]
    o_dt = SSD_WIDTH + SSD_XBC
    o_hg = o_dt + SSD_HEADS
    o_s5 = o_hg + HGRN_COLS
    w_ssd = jnp.concatenate(
        [w[:, :o_dt], jnp.pad(w[:, o_dt:o_hg], ((0, 0), (0, SSD_DT_PAD - SSD_HEADS)))], axis=1)
    lb = lb_all[l]
    bd, are, aim, cw = _s5_params(s5_lambda_re[l], s5_lambda_im[l], s5_log_step[l],
                                  s5_b_re[l], s5_b_im[l], s5_c_re[l], s5_c_im[l])
    return dict(
        pre_w=row(pre_norm_w[l]), post_w=row(post_norm_w[l]),
        w_ssd=w_ssd.astype(BF16), w_hg=w[:, o_hg:o_s5].astype(BF16), w_s5=w[:, o_s5:].astype(BF16),
        w_out=w_out[l].astype(BF16),
        conv_w=ssd_conv_w[l].astype(F32), conv_b=row(ssd_conv_b[l]),
        dt_bias=pad_heads(ssd_dt_bias[l]), a=pad_heads(-LOG2E * jnp.exp(ssd_a_log[l].astype(F32))),
        ssd_d=row(jnp.repeat(ssd_d[l].astype(F32), SSD_HEAD_DIM)), ssd_nw=row(ssd_norm_w[l]),
        log_lb=row(jnp.log(lb)), log1m_lb=row(jnp.log1p(-lb)), om_lb=row(1.0 - lb), hg_nw=row(hgrn_norm_w[l]),
        bd=bd, are=are, aim=aim, cw=cw, s5_d=row(s5_d[l]), glu_w=s5_glu_w[l].astype(BF16),
        glu_b=row(s5_glu_b[l]), s5_nw=row(s5_norm_w[l]),
    )


def kernel(x, pre_norm_w, post_norm_w, w_in, w_out, ssd_conv_w, ssd_conv_b, ssd_dt_bias, ssd_a_log, ssd_d, ssd_norm_w, hgrn_lower_bounds, hgrn_norm_w, s5_lambda_re, s5_lambda_im, s5_log_step, s5_b_re, s5_b_im, s5_c_re, s5_c_im, s5_d, s5_glu_w, s5_glu_b, s5_norm_w):
    lb_all = jnp.cumsum(jax.nn.softmax(hgrn_lower_bounds.astype(F32), axis=0), axis=0)
    lb_all = lb_all - lb_all[0:1]
    for l in range(w_in.shape[0]):
        p = _prep_layer(l, lb_all, pre_norm_w, post_norm_w, w_in, w_out, ssd_conv_w, ssd_conv_b,
                        ssd_dt_bias, ssd_a_log, ssd_d, ssd_norm_w, hgrn_norm_w, s5_lambda_re,
                        s5_lambda_im, s5_log_step, s5_b_re, s5_b_im, s5_c_re, s5_c_im, s5_d,
                        s5_glu_w, s5_glu_b, s5_norm_w)
        x = _layer(x, p)
    return x
```

```python
import jax
import jax.numpy as jnp
from jax import lax
from jax.experimental import pallas as pl
from jax.experimental.pallas import tpu as pltpu

F32 = jnp.float32
BF16 = jnp.bfloat16
EPS = 1e-6
LOG2E = 1.4426950408889634

D_MODEL = 1024
LANES = 128
SUBLANES = 8
MXU_DIM = 256

SSD_WIDTH = 1024
SSD_HEADS = 16
SSD_HEAD_DIM = 64
SSD_GROUPS = 2
SSD_STATE = 128
SSD_CHUNK = 128
SSD_CONV = 4
SSD_XBC = SSD_WIDTH + 2 * SSD_GROUPS * SSD_STATE
SSD_DT_PAD = LANES
SSD_COLS = SSD_WIDTH + SSD_XBC + SSD_DT_PAD
SSD_PAIRS = SSD_HEADS // 2

HGRN_WIDTH = 512
HGRN_HEADS = 4
HGRN_HEAD_DIM = 128
HGRN_CHUNK = 64
HGRN_COLS = 4 * HGRN_WIDTH
MIX_COLS = SSD_COLS + HGRN_COLS

S5_WIDTH = 512
S5_GROUPS = 32
S5_GROUP_SIZE = 16
S5_STATE = 64
S5_MIN_NEG = 1e-4
S5_COLS = 2 * S5_WIDTH
S5_TILES = S5_GROUPS // 2
S5_NSTATE = S5_TILES * MXU_DIM
S5_STRIP = 4

MIX_TT = 256
S5_TT = 32
VMEM_LIMIT = 56 * 1024 * 1024


def _sigmoid(x):
    return 0.5 * jnp.tanh(0.5 * x) + 0.5


def _silu(x):
    hx = 0.5 * x
    return hx * jnp.tanh(hx) + hx


def _dot(a, b):
    return jnp.dot(a, b, preferred_element_type=F32)


def _dot_nt(a, b):
    return lax.dot_general(a, b, (((1,), (1,)), ((), ())), preferred_element_type=F32)


def _dot_tn(a, b):
    return lax.dot_general(a, b, (((0,), (0,)), ((), ())), preferred_element_type=F32)


def _dot_exact01(m01, x):
    hi = x.astype(BF16)
    r1 = x - hi.astype(F32)
    mid = r1.astype(BF16)
    lo = (r1 - mid.astype(F32)).astype(BF16)
    return _dot(m01, hi) + _dot(m01, mid) + _dot(m01, lo)


def _iota(shape, axis):
    return lax.broadcasted_iota(jnp.int32, shape, axis)


def _cumsum8(x, rows_mod8):
    for sh in (1, 2, 4):
        x = x + jnp.where(rows_mod8 >= sh, pltpu.roll(x, sh, axis=0), 0.0)
    return x


def _rms_normed(x, w):
    ms = jnp.mean(x * x, axis=-1, keepdims=True)
    return x * lax.rsqrt(ms + EPS) * w


def _s5_kernel(x_ref, pw_ref, w_ref, bd_ref, are_ref, aim_ref, cw_ref, dsk_ref, gw_ref, gb_ref, nw_ref,
               o_ref, st_scr, in_scr, bu_scr, x_scr):
    nb, tt, d = x_ref.shape
    rows = tt * nb
    W = S5_WIDTH

    @pl.when(pl.program_id(0) == 0)
    def _():
        st_scr[...] = jnp.zeros(st_scr.shape, F32)

    h = _rms_normed(x_ref[...].reshape(rows, d), pw_ref[...]).astype(BF16)
    proj = _dot(h, w_ref[...])
    for b in range(nb):
        in_scr[:, b, :] = proj[b * tt:(b + 1) * tt, :]

    u = in_scr[:, :, 0:W].reshape(rows, W)
    ub = u.astype(BF16)
    tiles_per_k = MXU_DIM // (2 * S5_GROUP_SIZE)
    for j in range(S5_TILES):
        kt = j // tiles_per_k
        bu = _dot(ub[:, kt * MXU_DIM:(kt + 1) * MXU_DIM], bd_ref[j])
        bu_scr[:, :, j * MXU_DIM:(j + 1) * MXU_DIM] = bu.reshape(tt, nb, MXU_DIM)

    for s0 in range(0, S5_TILES, S5_STRIP):
        c0 = s0 * MXU_DIM
        sw = S5_STRIP * MXU_DIM
        a_re = [jnp.broadcast_to(are_ref[s0 + jj], (nb, LANES)) for jj in range(S5_STRIP)]
        a_im = [jnp.broadcast_to(aim_ref[s0 + jj], (nb, LANES)) for jj in range(S5_STRIP)]

        def step(t, st, c0=c0, sw=sw, a_re=a_re, a_im=a_im):
            b = bu_scr[t, :, c0:c0 + sw]
            parts = []
            for jj in range(S5_STRIP):
                o_re = jj * MXU_DIM
                o_im = o_re + LANES
                xr, xi = st[:, o_re:o_re + LANES], st[:, o_im:o_im + LANES]
                parts.append(a_re[jj] * xr - a_im[jj] * xi + b[:, o_re:o_re + LANES])
                parts.append(a_re[jj] * xi + a_im[jj] * xr + b[:, o_im:o_im + LANES])
            new = jnp.concatenate(parts, axis=1)
            x_scr[pl.ds(pl.multiple_of(t * nb, nb), nb), c0:c0 + sw] = new.astype(BF16)
            return new

        st_scr[:, c0:c0 + sw] = lax.fori_loop(0, tt, step, st_scr[:, c0:c0 + sw], unroll=True)

    ys = []
    tiles_per_n = S5_TILES // (W // MXU_DIM)
    for o in range(W // MXU_DIM):
        acc = jnp.zeros((rows, MXU_DIM), F32)
        for j in range(o * tiles_per_n, (o + 1) * tiles_per_n):
            acc = acc + _dot(x_scr[:, j * MXU_DIM:(j + 1) * MXU_DIM], cw_ref[j])
        ys.append(acc)
    y = jnp.concatenate(ys, axis=1) + dsk_ref[...] * u
    y = 0.5 * y * (1.0 + jnp.tanh(0.7978845608028654 * (y + 0.044715 * (y * y * y))))
    hg = _dot(y.astype(BF16), gw_ref[...]) + gb_ref[...]
    gate = in_scr[:, :, W:2 * W].reshape(rows, W)
    yy = hg[:, 0:W] * _sigmoid(hg[:, W:2 * W]) * _silu(gate)
    out = _rms_normed(yy, nw_ref[...]).reshape(tt, nb, W)
    for t in range(tt):
        o_ref[:, t, :] = out[t]


def _s5(x, l, pw, w, bd, are, aim, cw, dsk, gw, gb, nw):
    bsz, seqlen, d = x.shape
    tt = min(S5_TT, seqlen)
    const = lambda shape: pl.BlockSpec(shape, lambda i: (0,) * len(shape), pipeline_mode=pl.Buffered(1))
    layer = lambda shape: pl.BlockSpec((None,) + shape, lambda i: (l,) + (0,) * len(shape),
                                       pipeline_mode=pl.Buffered(1))
    return pl.pallas_call(
        _s5_kernel,
        grid=(seqlen // tt,),
        in_specs=[
            pl.BlockSpec((bsz, tt, d), lambda i: (0, i, 0)),
            const((1, d)),
            layer((d, S5_COLS)),
            const((S5_TILES, MXU_DIM, MXU_DIM)),
            const((S5_TILES, 1, LANES)),
            const((S5_TILES, 1, LANES)),
            const((S5_TILES, MXU_DIM, MXU_DIM)),
            const((1, S5_WIDTH)),
            layer((S5_WIDTH, 2 * S5_WIDTH)),
            const((1, 2 * S5_WIDTH)),
            const((1, S5_WIDTH)),
        ],
        out_specs=pl.BlockSpec((bsz, tt, S5_WIDTH), lambda i: (0, i, 0)),
        out_shape=jax.ShapeDtypeStruct((bsz, seqlen, S5_WIDTH), F32),
        scratch_shapes=[
            pltpu.VMEM((bsz, S5_NSTATE), F32),
            pltpu.VMEM((tt, bsz, S5_COLS), F32),
            pltpu.VMEM((tt, bsz, S5_NSTATE), F32),
            pltpu.VMEM((tt * bsz, S5_NSTATE), BF16),
        ],
        compiler_params=pltpu.CompilerParams(
            dimension_semantics=("arbitrary",), vmem_limit_bytes=VMEM_LIMIT),
        name="s5",
    )(x, pw, w, bd, are, aim, cw, dsk, gw, gb, nw)


def _ssd_prologue(p_cur, cw_ref, cb_ref, dtb_ref, conv_carry, act_scr, dt_scr, zs_scr):
    tt = act_scr.shape[0]
    zs_scr[...] = _silu(p_cur[:, 0:SSD_WIDTH])
    x = p_cur[:, SSD_WIDTH:SSD_WIDTH + SSD_XBC]
    first_row = _iota((SUBLANES, SSD_XBC), 0) == 0
    acc = cw_ref[0:1, :] * x
    for k in range(1, SSD_CONV):
        prev_last = conv_carry[k - 1:k, :]
        conv_carry[k - 1:k, :] = acc[tt - 1:tt, :]
        rolled = pltpu.roll(acc, 1, axis=0)
        head = jnp.where(first_row, prev_last, rolled[0:SUBLANES, :])
        acc = jnp.concatenate([head, rolled[SUBLANES:, :]], axis=0) + cw_ref[k:k + 1, :] * x
    act_scr[...] = _silu(acc + cb_ref[...])
    dtr = p_cur[:, SSD_WIDTH + SSD_XBC:SSD_COLS] + dtb_ref[...]
    dt_scr[...] = jnp.maximum(dtr, 0.0) + jnp.log(1.0 + jnp.exp(-jnp.abs(dtr)))


def _ssd_chunks(a_ref, dsk_ref, act_scr, dt_scr, y_buf, hst):
    tt = act_scr.shape[0]
    T = SSD_CHUNK
    row = _iota((T, T), 0)
    col = _iota((T, T), 1)
    tril = row >= col
    tri01 = jnp.where(tril, 1.0, 0.0).astype(BF16)
    lane_lo = col < SSD_HEAD_DIM
    lane_lo_row = _iota((1, LANES), 1) < SSD_HEAD_DIM
    pairs_per_group = SSD_PAIRS // SSD_GROUPS

    def chunk(c, carry):
        r0 = pl.multiple_of(c * T, T)
        dt_c = dt_scr[pl.ds(r0, T), :]
        a_cum = _dot_exact01(tri01, dt_c * a_ref[...])
        a_last = a_cum[T - 1:T, :]
        ea = jnp.exp2(a_cum)
        ea_last = jnp.exp2(a_last)
        a_cum_t = a_cum.T
        dt_t = dt_c.T
        w_t = (jnp.exp2(a_last - a_cum) * dt_c).T
        for g in range(SSD_GROUPS):
            bg = act_scr[pl.ds(r0, T), SSD_WIDTH + g * SSD_STATE:SSD_WIDTH + (g + 1) * SSD_STATE]
            c0 = SSD_WIDTH + SSD_GROUPS * SSD_STATE + g * SSD_STATE
            cg = act_scr[pl.ds(r0, T), c0:c0 + SSD_STATE]
            scores = _dot_nt(cg.astype(BF16), bg.astype(BF16))
            bg_t = bg.T
            for hp in range(pairs_per_group):
                lt = g * pairs_per_group + hp
                x2 = act_scr[pl.ds(r0, T), lt * LANES:(lt + 1) * LANES]
                h_prev = hst[lt]
                ypair = jnp.zeros((T, LANES), F32)
                spair = jnp.zeros((SSD_STATE, LANES), F32)
                for j in range(2):
                    hh = 2 * lt + j
                    lm = lane_lo if j == 0 else jnp.logical_not(lane_lo)
                    xm = jnp.where(lm, x2, 0.0).astype(BF16)
                    hm = jnp.where(lm, h_prev, 0.0).astype(BF16)
                    seg = a_cum[:, hh:hh + 1] - a_cum_t[hh:hh + 1, :]
                    decay = jnp.where(tril, jnp.exp2(jnp.minimum(seg, 0.0)), 0.0)
                    w = scores * decay * dt_t[hh:hh + 1, :]
                    ce = cg * ea[:, hh:hh + 1]
                    lhs = jnp.concatenate([w.astype(BF16), ce.astype(BF16)], axis=1)
                    rhs = jnp.concatenate([xm, hm], axis=0)
                    ypair = ypair + _dot(lhs, rhs)
                    spair = spair + _dot((bg_t * w_t[hh:hh + 1, :]).astype(BF16), xm)
                cd = jnp.where(lane_lo_row, ea_last[:, 2 * lt:2 * lt + 1], ea_last[:, 2 * lt + 1:2 * lt + 2])
                hst[lt] = h_prev * cd + spair
                y_buf[pl.ds(r0, T), lt * LANES:(lt + 1) * LANES] = (
                    ypair + dsk_ref[:, lt * LANES:(lt + 1) * LANES] * x2)
        return carry

    lax.fori_loop(0, tt // T, chunk, 0, unroll=True)


def _ssd_epilogue(nw_ref, y_buf, zs_scr):
    y = y_buf[...] * zs_scr[...]
    gw = SSD_WIDTH // SSD_GROUPS
    parts = []
    for g in range(SSD_GROUPS):
        yg = y[:, g * gw:(g + 1) * gw]
        ms = jnp.mean(yg * yg, axis=-1, keepdims=True)
        parts.append(yg * lax.rsqrt(ms + EPS))
    return (jnp.concatenate(parts, axis=1) * nw_ref[...]).astype(BF16)


def _hgrn_prologue(p_cur, loglb_ref, log1mlb_ref, omlb_ref, q_scr, k_scr, lc_scr, v_scr, gs_scr):
    tt = q_scr.shape[0]
    W = HGRN_WIDTH
    v_scr[...] = p_cur[:, SSD_COLS + 2 * W:SSD_COLS + 3 * W].astype(BF16)
    gs_scr[...] = _silu(p_cur[:, SSD_COLS + 3 * W:SSD_COLS + 4 * W])
    qr = p_cur[:, SSD_COLS:SSD_COLS + W]
    fr = p_cur[:, SSD_COLS + W:SSD_COLS + 2 * W]
    q_scr[...] = _silu(qr)
    log_sig = jnp.minimum(fr, 0.0) - jnp.log(1.0 + jnp.exp(-jnp.abs(fr)))
    t_b = log1mlb_ref[...] + log_sig
    t_a = loglb_ref[...]
    log_f = jnp.maximum(t_a, t_b) + jnp.log(1.0 + jnp.exp(-jnp.abs(t_a - t_b)))
    kk = omlb_ref[...] * _sigmoid(-fr)
    lc = _cumsum8(LOG2E * log_f, _iota((tt, W), 0) % SUBLANES)
    for h in range(HGRN_HEADS):
        lc_scr[h] = lc[:, h * LANES:(h + 1) * LANES]
        k_scr[h] = kk[:, h * LANES:(h + 1) * LANES]


def _hgrn_chunks(v_scr, q_scr, k_scr, lc_scr, s_scr, o_buf):
    tt = q_scr.shape[0]
    C = HGRN_CHUNK
    nsub = C // SUBLANES
    r8 = _iota((SUBLANES, LANES), 0)
    rr = _iota((C, C), 0)
    cc = _iota((C, C), 1)
    rl = _iota((C, LANES), 0)
    rsel = jnp.where(_iota((SUBLANES * LANES, C), 0) // LANES == _iota((SUBLANES * LANES, C), 1) % SUBLANES,
                     1.0, 0.0).astype(BF16)
    diag_mask = ((rr // SUBLANES) == (cc // SUBLANES)) & (rr >= cc)

    def bcast_rows(src, idx, reps):
        return jnp.concatenate(
            [jnp.broadcast_to(src[i:i + 1, :], (reps, LANES)) for i in idx], axis=0)

    def chunk(c, carry):
        r0 = pl.multiple_of(c * C, C)
        for h in range(HGRN_HEADS):
            ls = slice(h * LANES, (h + 1) * LANES)
            q = q_scr[pl.ds(r0, C), ls]
            k = k_scr[h, pl.ds(r0, C), :]
            v = v_scr[pl.ds(r0, C), ls]
            lc_c = lc_scr[h, pl.ds(r0, C), :]
            tot = lc_scr[h, pl.ds(r0 + SUBLANES - 1, nsub, stride=SUBLANES), :]
            ci = _cumsum8(tot, r8)
            ce = ci - tot
            cum = lc_c + bcast_rows(ce, range(nsub), SUBLANES)
            last = ci[nsub - 1:nsub, :]

            s_t = s_scr[h]
            o_c = _dot_nt((q * jnp.exp2(cum)).astype(BF16), s_t.astype(BF16))
            kd = (k * jnp.exp2(last - cum)).astype(BF16)
            s_scr[h] = s_t * jnp.exp2(last) + _dot_tn(v, kd)

            ps = []
            for s in range(SUBLANES):
                ks = jnp.concatenate(
                    [jnp.broadcast_to(k_scr[h, pl.ds(r0 + b * SUBLANES + s, 1), :], (SUBLANES, LANES))
                     for b in range(nsub)], axis=0)
                lcs = jnp.concatenate(
                    [jnp.broadcast_to(lc_scr[h, pl.ds(r0 + b * SUBLANES + s, 1), :], (SUBLANES, LANES))
                     for b in range(nsub)], axis=0)
                ps.append((q * ks * jnp.exp2(jnp.minimum(lc_c - lcs, 0.0))).astype(BF16))
            attn = jnp.where(diag_mask, _dot(jnp.concatenate(ps, axis=1), rsel), 0.0)
            for m in (8, 16, 32):
                nblk = C // (2 * m)
                cmid = bcast_rows(ci, [(2 * j + 1) * (m // SUBLANES) - 1 for j in range(nblk)], 2 * m)
                second = (rl % (2 * m)) >= m
                xe = (jnp.where(second, q, k) * jnp.exp2(-jnp.abs(cum - cmid))).astype(BF16)
                keep = ((rr % (2 * m)) >= m) & ((cc % (2 * m)) < m) & ((rr // (2 * m)) == (cc // (2 * m)))
                attn = attn + jnp.where(keep, _dot_nt(xe, xe), 0.0)
            o_buf[pl.ds(r0, C), ls] = o_c + _dot(attn.astype(BF16), v)
        return carry

    lax.fori_loop(0, tt // C, chunk, 0, unroll=True)


def _hgrn_epilogue(nw_ref, o_buf, gs_scr):
    o = o_buf[...]
    parts = []
    for h in range(HGRN_HEADS):
        oh = o[:, h * LANES:(h + 1) * LANES]
        ms = jnp.mean(oh * oh, axis=-1, keepdims=True)
        parts.append(oh * lax.rsqrt(ms + EPS))
    return (jnp.concatenate(parts, axis=1) * nw_ref[...] * gs_scr[...]).astype(BF16)


def _mix_kernel(x_ref, xn_ref, ys5_ref, pw_ref, wssd_ref, whg_ref,
                cw_ref, cb_ref, dtb_ref, a_ref, dsk_ref, snw_ref,
                loglb_ref, log1mlb_ref, omlb_ref, hnw_ref, wo_ref, qw_ref, o_ref,
                p_scr, xbc_buf, act_scr, dt_scr, y_buf, hst, q_scr, k_scr, lc_scr, s_scr, o_buf,
                zs_scr, v_scr, gs_scr):
    b = pl.program_id(0)
    i = pl.program_id(1)
    cur = (b * pl.num_programs(1) + i) % 2

    def project(src_ref, dst):
        h = _rms_normed(src_ref[0], pw_ref[...]).astype(BF16)
        dst[:, 0:SSD_COLS] = _dot(h, wssd_ref[...])
        dst[:, SSD_COLS:MIX_COLS] = _dot(h, whg_ref[...])

    @pl.when(i == 0)
    def _():
        xbc_buf[...] = jnp.zeros((SUBLANES, SSD_XBC), F32)
        hst[...] = jnp.zeros(hst.shape, F32)
        s_scr[...] = jnp.zeros(s_scr.shape, F32)

    @pl.when((i == 0) & (b == 0))
    def _():
        project(x_ref, p_scr.at[0])

    p_cur = p_scr.at[cur]
    _ssd_prologue(p_cur, cw_ref, cb_ref, dtb_ref, xbc_buf, act_scr, dt_scr, zs_scr)
    _hgrn_prologue(p_cur, loglb_ref, log1mlb_ref, omlb_ref, q_scr, k_scr, lc_scr, v_scr, gs_scr)
    _ssd_chunks(a_ref, dsk_ref, act_scr, dt_scr, y_buf, hst)
    _hgrn_chunks(v_scr, q_scr, k_scr, lc_scr, s_scr, o_buf)
    y_ssd = _ssd_epilogue(snw_ref, y_buf, zs_scr)
    y_hg = _hgrn_epilogue(hnw_ref, o_buf, gs_scr)
    m = (_dot(y_ssd, wo_ref[0:SSD_WIDTH, :])
         + _dot(y_hg, wo_ref[SSD_WIDTH:SSD_WIDTH + HGRN_WIDTH, :])
         + _dot(ys5_ref[0].astype(BF16), wo_ref[SSD_WIDTH + HGRN_WIDTH:, :]))
    o_ref[0] = x_ref[0] + _rms_normed(m, qw_ref[...])
    project(xn_ref, p_scr.at[1 - cur])


def _mix(x, ys5, p):
    bsz, seqlen, d = x.shape
    tt = min(MIX_TT, seqlen)
    nt = seqlen // tt
    const = lambda shape: pl.BlockSpec(shape, lambda b, i: (0,) * len(shape), pipeline_mode=pl.Buffered(1))
    l = p["layer"]
    layer = lambda shape: pl.BlockSpec((None,) + shape, lambda b, i: (l,) + (0,) * len(shape),
                                       pipeline_mode=pl.Buffered(1))
    return pl.pallas_call(
        _mix_kernel,
        grid=(bsz, nt),
        in_specs=[
            pl.BlockSpec((1, tt, d), lambda b, i: (b, i, 0)),
            pl.BlockSpec((1, tt, d), lambda b, i: (jnp.minimum(b + (i + 1) // nt, bsz - 1), (i + 1) % nt, 0)),
            pl.BlockSpec((1, tt, S5_WIDTH), lambda b, i: (b, i, 0)),
            const((1, d)),
            layer((d, SSD_COLS)),
            layer((d, HGRN_COLS)),
            const((SSD_CONV, SSD_XBC)),
            const((1, SSD_XBC)),
            const((1, SSD_DT_PAD)),
            const((1, SSD_DT_PAD)),
            const((1, SSD_WIDTH)),
            const((1, SSD_WIDTH)),
            const((1, HGRN_WIDTH)), const((1, HGRN_WIDTH)), const((1, HGRN_WIDTH)), const((1, HGRN_WIDTH)),
            layer((2 * d, d)),
            const((1, d)),
        ],
        out_specs=pl.BlockSpec((1, tt, d), lambda b, i: (b, i, 0)),
        out_shape=jax.ShapeDtypeStruct((bsz, seqlen, d), F32),
        scratch_shapes=[
            pltpu.VMEM((2, tt, MIX_COLS), F32),
            pltpu.VMEM((SUBLANES, SSD_XBC), F32),
            pltpu.VMEM((tt, SSD_XBC), F32),
            pltpu.VMEM((tt, SSD_DT_PAD), F32),
            pltpu.VMEM((tt, SSD_WIDTH), F32),
            pltpu.VMEM((SSD_PAIRS, SSD_STATE, LANES), F32),
            pltpu.VMEM((tt, HGRN_WIDTH), F32),
            pltpu.VMEM((HGRN_HEADS, tt, LANES), F32),
            pltpu.VMEM((HGRN_HEADS, tt, LANES), F32),
            pltpu.VMEM((HGRN_HEADS, HGRN_HEAD_DIM, HGRN_HEAD_DIM), F32),
            pltpu.VMEM((tt, HGRN_WIDTH), F32),
            pltpu.VMEM((tt, SSD_WIDTH), F32),
            pltpu.VMEM((tt, HGRN_WIDTH), BF16),
            pltpu.VMEM((tt, HGRN_WIDTH), F32),
        ],
        compiler_params=pltpu.CompilerParams(
            dimension_semantics=("arbitrary", "arbitrary"), vmem_limit_bytes=VMEM_LIMIT),
        name="mix",
    )(x, x, ys5, p["pre_w"], p["w_ssd"], p["w_hg"],
      p["conv_w"], p["conv_b"], p["dt_bias"], p["a"], p["ssd_d"], p["ssd_nw"],
      p["log_lb"], p["log1m_lb"], p["om_lb"], p["hg_nw"], p["w_out"], p["post_w"])


def _s5_params(lam_re, lam_im, log_step, b_re, b_im, c_re, c_im):
    lr = jnp.minimum(lam_re.astype(F32), -S5_MIN_NEG)
    li = lam_im.astype(F32)
    step = jnp.exp(log_step.astype(F32))[:, None]
    mag = jnp.exp(lr * step)
    ang = li * step
    ab_re, ab_im = mag * jnp.cos(ang), mag * jnp.sin(ang)
    den = lr * lr + li * li
    num_re, num_im = ab_re - 1.0, ab_im
    co_re = (num_re * lr + num_im * li) / den
    co_im = (num_im * lr - num_re * li) / den
    bb_re = co_re[..., None] * b_re - co_im[..., None] * b_im
    bb_im = co_re[..., None] * b_im + co_im[..., None] * b_re
    P, H = S5_STATE, S5_GROUP_SIZE
    pair = lambda t: t.reshape(S5_TILES, 2, *t.shape[1:])
    eye2 = jnp.eye(2, dtype=F32)
    blk_re = jnp.einsum("jgph,gk->jghkp", pair(bb_re), eye2).reshape(S5_TILES, 2 * H, 2 * P)
    blk_im = jnp.einsum("jgph,gk->jghkp", pair(bb_im), eye2).reshape(S5_TILES, 2 * H, 2 * P)
    blk = jnp.concatenate([blk_re, blk_im], axis=-1)
    tiles_per_k = MXU_DIM // (2 * H)
    slot = jax.nn.one_hot(jnp.arange(S5_TILES) % tiles_per_k, tiles_per_k, dtype=F32)
    bd = jnp.einsum("js,jrc->jsrc", slot, blk).reshape(S5_TILES, MXU_DIM, MXU_DIM)
    out_re = jnp.einsum("jghp,gk->jgpkh", pair(c_re.astype(F32)), eye2).reshape(S5_TILES, 2 * P, 2 * H)
    out_im = jnp.einsum("jghp,gk->jgpkh", pair(c_im.astype(F32)), eye2).reshape(S5_TILES, 2 * P, 2 * H)
    oblk = jnp.concatenate([out_re, -out_im], axis=1)
    cw = jnp.einsum("js,jrc->jrsc", slot, oblk).reshape(S5_TILES, MXU_DIM, MXU_DIM)
    are = pair(ab_re).reshape(S5_TILES, 1, 2 * P)
    aim = pair(ab_im).reshape(S5_TILES, 1, 2 * P)
    return bd.astype(BF16), are, aim, cw.astype(BF16)


def _layer(x, p):
    y_s5 = _s5(x, p["layer"], p["pre_w"], p["w_s5"], p["bd"], p["are"], p["aim"], p["cw"], p["s5_d"],
               p["glu_w"], p["glu_b"], p["s5_nw"])
    return _mix(x, y_s5, p)


def _prep_layer(l, lb_all, pre_norm_w, post_norm_w, w_in, w_out, ssd_conv_w, ssd_conv_b, ssd_dt_bias,
                ssd_a_log, ssd_d, ssd_norm_w, hgrn_norm_w, s5_lambda_re, s5_lambda_im, s5_log_step,
                s5_b_re, s5_b_im, s5_c_re, s5_c_im, s5_d, s5_glu_w, s5_glu_b, s5_norm_w):
    row = lambda v: v.astype(F32).reshape(1, -1)
    pad_heads = lambda v: jnp.pad(v.astype(F32), (0, SSD_DT_PAD - SSD_HEADS)).reshape(1, -1)
    o_dt = SSD_WIDTH + SSD_XBC
    o_hg = o_dt + SSD_HEADS
    o_s5 = o_hg + HGRN_COLS
    w_ssd = jnp.concatenate(
        [w_in[:, :, :o_dt], jnp.pad(w_in[:, :, o_dt:o_hg], ((0, 0), (0, 0), (0, SSD_DT_PAD - SSD_HEADS)))], axis=-1)
    lb = lb_all[l]
    bd, are, aim, cw = _s5_params(s5_lambda_re[l], s5_lambda_im[l], s5_log_step[l],
                                  s5_b_re[l], s5_b_im[l], s5_c_re[l], s5_c_im[l])
    return dict(
        pre_w=row(pre_norm_w[l]), post_w=row(post_norm_w[l]),
        layer=l, w_ssd=w_ssd.astype(BF16), w_hg=w_in[:, :, o_hg:o_s5].astype(BF16),
        w_s5=w_in[:, :, o_s5:].astype(BF16), w_out=w_out.astype(BF16),
        conv_w=ssd_conv_w[l].astype(F32), conv_b=row(ssd_conv_b[l]),
        dt_bias=pad_heads(ssd_dt_bias[l]), a=pad_heads(-LOG2E * jnp.exp(ssd_a_log[l].astype(F32))),
        ssd_d=row(jnp.repeat(ssd_d[l].astype(F32), SSD_HEAD_DIM)), ssd_nw=row(ssd_norm_w[l]),
        log_lb=row(jnp.log(lb)), log1m_lb=row(jnp.log1p(-lb)), om_lb=row(1.0 - lb), hg_nw=row(hgrn_norm_w[l]),
        bd=bd, are=are, aim=aim, cw=cw, s5_d=row(s5_d[l]), glu_w=s5_glu_w.astype(BF16),
        glu_b=row(s5_glu_b[l]), s5_nw=row(s5_norm_w[l]),
    )


def kernel(x, pre_norm_w, post_norm_w, w_in, w_out, ssd_conv_w, ssd_conv_b, ssd_dt_bias, ssd_a_log, ssd_d, ssd_norm_w, hgrn_lower_bounds, hgrn_norm_w, s5_lambda_re, s5_lambda_im, s5_log_step, s5_b_re, s5_b_im, s5_c_re, s5_c_im, s5_d, s5_glu_w, s5_glu_b, s5_norm_w):
    lb_all = jnp.cumsum(jax.nn.softmax(hgrn_lower_bounds.astype(F32), axis=0), axis=0)
    lb_all = lb_all - lb_all[0:1]
    for l in range(w_in.shape[0]):
        p = _prep_layer(l, lb_all, pre_norm_w, post_norm_w, w_in, w_out, ssd_conv_w, ssd_conv_b,
                        ssd_dt_bias, ssd_a_log, ssd_d, ssd_norm_w, hgrn_norm_w, s5_lambda_re,
                        s5_lambda_im, s5_log_step, s5_b_re, s5_b_im, s5_c_re, s5_c_im, s5_d,
                        s5_glu_w, s5_glu_b, s5_norm_w)
        x = _layer(x, p)
    return x
```
